```python
import math
import jax, jax.numpy as jnp
from jax import lax
import numpy as np

D_MODEL = 1024
BATCH = 8
SEQ = 4096
DEPTH = 1

CHUNK = 64
D_FF = 2816
POOL_WIDTH = D_MODEL // 2
POOL_WINDOWS = (2, 4, 8, 16)
N_POOL_GROUPS = len(POOL_WINDOWS)
POOL_GROUP = POOL_WIDTH // N_POOL_GROUPS
SSM_WIDTH = D_MODEL // 2
SSM_GROUP = 16
N_SSM_GROUPS = SSM_WIDTH // SSM_GROUP
SSM_STATE = 64
N_SUBLAYERS = 3
N_MOD = 3
IN_WIDTH = POOL_WIDTH + SSM_WIDTH + 2 * D_MODEL
EPS = 1e-6
DT_MIN = 1e-3
DT_MAX = 1e-1

kernel_name = "hybrid_pool_s5_macaron_adaln"


def rms_norm(x, g):
    xf = x.astype(jnp.float32)
    y = xf * lax.rsqrt(jnp.mean(xf * xf, axis=-1, keepdims=True) + EPS)
    return (y * g.astype(jnp.float32)).astype(x.dtype)


def modulate(h, shift, scale):
    return h * (1 + scale[:, None, :]) + shift[:, None, :]


def swiglu(h, w_in, w_out):
    a, b = jnp.split(h @ w_in, 2, axis=-1)
    return (jax.nn.silu(a) * b) @ w_out


def multiscale_pool(u, pool_w, pool_b, pool_scale):
    B, T, _ = u.shape
    ug = u.reshape(B, T, N_POOL_GROUPS, POOL_GROUP)
    cs = jnp.cumsum(ug.astype(jnp.float32), axis=1)
    pos = jnp.arange(T)
    means = []
    for k, w in enumerate(POOL_WINDOWS):
        csk = cs[:, :, k]
        prev = jnp.pad(csk, ((0, 0), (w, 0), (0, 0)))[:, :T]
        cnt = jnp.minimum(pos + 1, w).astype(jnp.float32)[None, :, None]
        means.append((csk - prev) / cnt)
    mean = jnp.stack(means, axis=2).astype(u.dtype)
    z = mean - ug
    z = jnp.einsum('btgc,gcd->btgd', z, pool_w) + pool_b.reshape(N_POOL_GROUPS, POOL_GROUP)
    return z.reshape(B, T, POOL_WIDTH) * pool_scale


def _ssm_combine(left, right):
    ar_l, ai_l, br_l, bi_l = left
    ar_r, ai_r, br_r, bi_r = right
    ar = ar_r * ar_l - ai_r * ai_l
    ai = ar_r * ai_l + ai_r * ar_l
    br = ar_r * br_l - ai_r * bi_l + br_r
    bi = ar_r * bi_l + ai_r * br_l + bi_r
    return (ar, ai, br, bi)


def s5_mixer(u, lam_re_log, lam_im, log_dt, b_re, b_im, c_re, c_im, d_skip, w_glu, b_glu):
    B, T, _ = u.shape
    f32 = jnp.float32
    uf = u.astype(f32).reshape(B, T, N_SSM_GROUPS, SSM_GROUP)
    lr = -jnp.exp(lam_re_log.astype(f32))
    li = lam_im.astype(f32)
    dt = jnp.exp(log_dt.astype(f32))[:, None]
    mag = jnp.exp(lr * dt)
    ang = li * dt
    ab_re = mag * jnp.cos(ang)
    ab_im = mag * jnp.sin(ang)
    num_re = ab_re - 1.0
    num_im = ab_im
    den = lr * lr + li * li
    f_re = (num_re * lr + num_im * li) / den
    f_im = (num_im * lr - num_re * li) / den
    br = b_re.astype(f32)
    bi = b_im.astype(f32)
    bb_re = f_re[..., None] * br - f_im[..., None] * bi
    bb_im = f_re[..., None] * bi + f_im[..., None] * br
    bu_re = jnp.einsum('btgh,gnh->btgn', uf, bb_re)
    bu_im = jnp.einsum('btgh,gnh->btgn', uf, bb_im)
    a_re = jnp.broadcast_to(ab_re[None, None], (1, T, N_SSM_GROUPS, SSM_STATE))
    a_im = jnp.broadcast_to(ab_im[None, None], (1, T, N_SSM_GROUPS, SSM_STATE))
    _, _, s_re, s_im = lax.associative_scan(_ssm_combine, (a_re, a_im, bu_re, bu_im), axis=1)
    y = (jnp.einsum('btgn,ghn->btgh', s_re, c_re.astype(f32))
         - jnp.einsum('btgn,ghn->btgh', s_im, c_im.astype(f32)))
    y = y.reshape(B, T, SSM_WIDTH) + d_skip.astype(f32) * uf.reshape(B, T, SSM_WIDTH)
    y = jax.nn.gelu(y.astype(u.dtype), approximate=False)
    val, gate = jnp.split(y @ w_glu + b_glu, 2, axis=-1)
    return val * jax.nn.sigmoid(gate)


def setup_inputs(seed: int = 0) -> dict:
    key = jax.random.key(seed)
    ks = jax.random.split(key, 32)
    L, D, F = DEPTH, D_MODEL, D_FF
    G, H, N = N_SSM_GROUPS, SSM_GROUP, SSM_STATE
    nrm = lambda k, shape, s: jax.random.normal(k, shape, jnp.float32) * s
    n_idx = jnp.arange(N, dtype=jnp.float32)[None, :]
    return {
        "x": nrm(ks[0], (BATCH, SEQ, D), 1.0),
        "c": nrm(ks[1], (BATCH, D), 1.0),
        "w_ada": nrm(ks[2], (L, D, N_SUBLAYERS * N_MOD * D), 0.5 * D ** -0.5),
        "b_ada": nrm(ks[3], (L, N_SUBLAYERS * N_MOD * D), 0.01),
        "g_ffn1": 1.0 + nrm(ks[4], (L, D), 0.01),
        "w_ffn1_in": nrm(ks[5], (L, D, 2 * F), D ** -0.5),
        "w_ffn1_out": nrm(ks[6], (L, F, D), F ** -0.5),
        "g_mix": 1.0 + nrm(ks[7], (L, D), 0.01),
        "w_in": nrm(ks[8], (L, D, IN_WIDTH), D ** -0.5),
        "pool_w": nrm(ks[9], (L, N_POOL_GROUPS, POOL_GROUP, POOL_GROUP), POOL_GROUP ** -0.5),
        "pool_b": nrm(ks[10], (L, POOL_WIDTH), 0.01),
        "pool_scale": 1.0 + nrm(ks[11], (L, POOL_WIDTH), 0.05),
        "w_pool_up": nrm(ks[12], (L, POOL_WIDTH, D), POOL_WIDTH ** -0.5),
        "ssm_lam_re_log": jnp.log(0.5) + nrm(ks[13], (L, G, N), 0.01),
        "ssm_lam_im": math.pi * n_idx + nrm(ks[14], (L, G, N), 0.01),
        "ssm_log_dt": jax.random.uniform(ks[15], (L, G), jnp.float32, math.log(DT_MIN), math.log(DT_MAX)),
        "ssm_b_re": nrm(ks[16], (L, G, N, H), (2 * H) ** -0.5),
        "ssm_b_im": nrm(ks[17], (L, G, N, H), (2 * H) ** -0.5),
        "ssm_c_re": nrm(ks[18], (L, G, H, N), N ** -0.5),
        "ssm_c_im": nrm(ks[19], (L, G, H, N), N ** -0.5),
        "ssm_d": nrm(ks[20], (L, SSM_WIDTH), 1.0),
        "w_glu": nrm(ks[21], (L, SSM_WIDTH, 2 * SSM_WIDTH), SSM_WIDTH ** -0.5),
        "b_glu": nrm(ks[22], (L, 2 * SSM_WIDTH), 0.01),
        "w_ssm_up": nrm(ks[23], (L, SSM_WIDTH, D), SSM_WIDTH ** -0.5),
        "w_out": nrm(ks[24], (L, D, D), D ** -0.5),
        "g_ffn2": 1.0 + nrm(ks[25], (L, D), 0.01),
        "w_ffn2_in": nrm(ks[26], (L, D, 2 * F), D ** -0.5),
        "w_ffn2_out": nrm(ks[27], (L, F, D), F ** -0.5),
        "g_final": 1.0 + nrm(ks[28], (D,), 0.01),
    }


def reference(x, c, w_ada, b_ada, g_ffn1, w_ffn1_in, w_ffn1_out, g_mix, w_in,
              pool_w, pool_b, pool_scale, w_pool_up,
              ssm_lam_re_log, ssm_lam_im, ssm_log_dt, ssm_b_re, ssm_b_im, ssm_c_re, ssm_c_im, ssm_d,
              w_glu, b_glu, w_ssm_up, w_out, g_ffn2, w_ffn2_in, w_ffn2_out, g_final):
    B = x.shape[0]
    split_pts = (POOL_WIDTH, POOL_WIDTH + SSM_WIDTH, POOL_WIDTH + SSM_WIDTH + D_MODEL)
    for l in range(DEPTH):
        mod = (jax.nn.silu(c) @ w_ada[l] + b_ada[l]).reshape(B, N_SUBLAYERS, N_MOD, D_MODEL)

        h = modulate(rms_norm(x, g_ffn1[l]), mod[:, 0, 0], mod[:, 0, 1])
        x = x + 0.5 * mod[:, 0, 2][:, None, :] * swiglu(h, w_ffn1_in[l], w_ffn1_out[l])

        h = modulate(rms_norm(x, g_mix[l]), mod[:, 1, 0], mod[:, 1, 1])
        u_pool, u_ssm, gl_pool, gl_ssm = jnp.split(h @ w_in[l], split_pts, axis=-1)
        y_pool = multiscale_pool(u_pool, pool_w[l], pool_b[l], pool_scale[l]) @ w_pool_up[l]
        y_ssm = s5_mixer(u_ssm, ssm_lam_re_log[l], ssm_lam_im[l], ssm_log_dt[l],
                         ssm_b_re[l], ssm_b_im[l], ssm_c_re[l], ssm_c_im[l], ssm_d[l],
                         w_glu[l], b_glu[l]) @ w_ssm_up[l]
        merged = jax.nn.sigmoid(gl_pool) * y_pool + jax.nn.sigmoid(gl_ssm) * y_ssm
        x = x + mod[:, 1, 2][:, None, :] * (merged @ w_out[l])

        h = modulate(rms_norm(x, g_ffn2[l]), mod[:, 2, 0], mod[:, 2, 1])
        x = x + 0.5 * mod[:, 2, 2][:, None, :] * swiglu(h, w_ffn2_in[l], w_ffn2_out[l])
    return rms_norm(x, g_final)
```

```python
import functools
import math

import jax
import jax.numpy as jnp
from jax import lax
from jax.experimental import pallas as pl
from jax.experimental.pallas import tpu as pltpu

D_MODEL = 1024
D_FF = 2816
POOL_WIDTH = 512
POOL_WINDOWS = (2, 4, 8, 16)
POOL_GROUP = 128
SSM_WIDTH = 512
SSM_GROUP = 16
N_SSM_GROUPS = 32
SSM_STATE = 64
N_STATES = N_SSM_GROUPS * SSM_STATE
EPS = 1e-6

F32 = jnp.float32
BF16 = jnp.bfloat16

SUBLANES = 8
LANES = 128
MXU_DIM = 256
VMEM_LIMIT_BYTES = 60000 * 1024

FFN_TM = 512
FFN_FC = MXU_DIM
MIX_TT = 64
MIX_RS = 256
HALO_FRAMES = 16
SCAN_LANES = 512
SCAN_UNROLL = 8


def _resident(shape):
    nd = len(shape)
    return pl.BlockSpec(shape, lambda *_: (0,) * nd, pipeline_mode=pl.Buffered(1))


def _rms_norm(x, g):
    ms = jnp.mean(x * x, axis=-1, keepdims=True)
    return x * lax.rsqrt(ms + EPS) * g


def _dot(a, b):
    return jnp.dot(a, b, preferred_element_type=F32)


def _adaln_kernel(c_ref, w_ref, b_ref, o_ref):
    c = c_ref[...]
    s = c * jax.nn.sigmoid(c)
    o_ref[...] = _dot(s.astype(BF16), w_ref[...].astype(BF16)) + b_ref[...]


def _adaln(c, w, b):
    bsz, d = c.shape
    n = w.shape[1]
    tn = n // 8
    return pl.pallas_call(
        _adaln_kernel,
        grid=(n // tn,),
        in_specs=[
            pl.BlockSpec((bsz, d), lambda j: (0, 0)),
            pl.BlockSpec((d, tn), lambda j: (0, j)),
            pl.BlockSpec((1, tn), lambda j: (0, j)),
        ],
        out_specs=pl.BlockSpec((bsz, tn), lambda j: (0, j)),
        out_shape=jax.ShapeDtypeStruct((bsz, n), F32),
        compiler_params=pltpu.CompilerParams(dimension_semantics=("arbitrary",)),
        name="adaln",
    )(c, w, b.reshape(1, n))


def _ssm_prep_kernel(lre_ref, lim_ref, ldt_ref, bre_ref, bim_ref,
                     are_ref, aim_ref, bbre_ref, bbim_ref):
    lr = -jnp.exp(lre_ref[...])
    li = lim_ref[...]
    dt = jnp.exp(ldt_ref[...])
    mag = jnp.exp(lr * dt)
    ang = li * dt
    ab_re = mag * jnp.cos(ang)
    ab_im = mag * jnp.sin(ang)
    num_re = ab_re - 1.0
    num_im = ab_im
    den = lr * lr + li * li
    f_re = (num_re * lr + num_im * li) / den
    f_im = (num_im * lr - num_re * li) / den
    br = bre_ref[...]
    bi = bim_ref[...]
    are_ref[...] = ab_re
    aim_ref[...] = ab_im
    bbre_ref[...] = f_re * br - f_im * bi
    bbim_ref[...] = f_re * bi + f_im * br


def _ssm_prep(lam_re_log, lam_im, log_dt, b_re, b_im):
    g, n, h = b_re.shape
    rows = g * n
    col = lambda a: a.reshape(rows, 1)
    ldt = jnp.broadcast_to(log_dt[:, None], (g, n))
    full = lambda shape: pl.BlockSpec(shape, lambda: (0,) * len(shape))
    return pl.pallas_call(
        _ssm_prep_kernel,
        in_specs=[full((rows, 1))] * 3 + [full((rows, h))] * 2,
        out_specs=[full((rows, 1))] * 2 + [full((rows, h))] * 2,
        out_shape=[jax.ShapeDtypeStruct((rows, 1), F32)] * 2
        + [jax.ShapeDtypeStruct((rows, h), F32)] * 2,
        name="ssm_prep",
    )(col(lam_re_log), col(lam_im), col(ldt), b_re.reshape(rows, h), b_im.reshape(rows, h))


def _ffn_kernel(x_ref, mod_ref, g_ref, wa_ref, wb_ref, wo_ref, gfin_ref, o_ref,
                h_ref, acc_ref, *, sublayer, final_norm):
    x = x_ref[...]
    shift = mod_ref[3 * sublayer + 0:3 * sublayer + 1, :]
    scale = mod_ref[3 * sublayer + 1:3 * sublayer + 2, :]
    gate = mod_ref[3 * sublayer + 2:3 * sublayer + 3, :]
    h = _rms_norm(x, g_ref[...]) * (1.0 + scale) + shift
    h_ref[...] = h.astype(BF16)
    acc_ref[...] = jnp.zeros_like(acc_ref)

    def chunk(c, carry):
        hb = h_ref[...]
        a = _dot(hb, wa_ref[c])
        b = _dot(hb, wb_ref[c])
        gl = (a * jax.nn.sigmoid(a) * b).astype(BF16)
        acc_ref[...] += _dot(gl, wo_ref[c])
        return carry

    lax.fori_loop(0, wa_ref.shape[0], chunk, 0)
    out = x + (0.5 * gate) * acc_ref[...]
    if final_norm:
        out = _rms_norm(out, gfin_ref[...])
    o_ref[...] = out


def _ffn(x, mod, g, w_in, w_out, g_final, *, sublayer, time_major_in, final_norm, bsz, seq):
    d, f, fc, tm = D_MODEL, D_FF, FFN_FC, FFN_TM
    nc = f // fc
    w3 = w_in.astype(BF16).reshape(d, 2, nc, fc).transpose(1, 2, 0, 3)
    wa, wb = w3[0], w3[1]
    wo = w_out.astype(BF16).reshape(nc, fc, d)
    bt_spec = pl.BlockSpec((None, tm, d), lambda b, i: (b, i, 0))
    tm_spec = pl.BlockSpec((tm, d), lambda b, i: (i, b))
    if time_major_in:
        in_spec, out_spec = tm_spec, bt_spec
        out_shape = jax.ShapeDtypeStruct((bsz, seq, d), F32)
    else:
        in_spec, out_spec = bt_spec, tm_spec
        out_shape = jax.ShapeDtypeStruct((seq, bsz * d), F32)
    return pl.pallas_call(
        functools.partial(_ffn_kernel, sublayer=sublayer, final_norm=final_norm),
        grid=(bsz, seq // tm),
        in_specs=[
            in_spec,
            pl.BlockSpec((None, 9, d), lambda b, i: (b, 0, 0)),
            _resident((1, d)),
            _resident((nc, d, fc)),
            _resident((nc, d, fc)),
            _resident((nc, fc, d)),
            _resident((1, d)),
        ],
        out_specs=out_spec,
        out_shape=out_shape,
        scratch_shapes=[pltpu.VMEM((tm, d), BF16), pltpu.VMEM((tm, d), F32)],
        compiler_params=pltpu.CompilerParams(
            dimension_semantics=("arbitrary", "arbitrary"),
            vmem_limit_bytes=VMEM_LIMIT_BYTES),
        name=f"ffn{sublayer}",
    )(x, mod, g.reshape(1, d), wa, wb, wo, g_final.reshape(1, d))


def _mixer_kernel(x_ref, mod_ref, g_ref, win_ref, pw_ref, pb_ref, ps_ref, wpu_ref,
                  btre_ref, btim_ref, are_ref, aim_ref, ctre_ref, ctim_ref, dsk_ref,
                  wglu_ref, bglu_ref, wsu_ref, wout_ref, o_ref,
                  hist_ref, state_ref, bu_ref, merged_ref, gls_ref, us_ref):
    step = pl.program_id(0)
    rows = x_ref.shape[0]
    rs = MIX_RS
    halo = HALO_FRAMES * SUBLANES

    @pl.when(step == 0)
    def _():
        state_ref[...] = jnp.zeros_like(state_ref)
        hist_ref[0:halo, :] = jnp.zeros((halo, POOL_WIDTH), F32)

    def phase1(k, carry):
        r0 = pl.multiple_of(k * rs, rs)
        x = x_ref[pl.ds(r0, rs), :]
        h = _rms_norm(x, g_ref[...]) * (1.0 + mod_ref[1]) + mod_ref[0]
        hb = h.astype(BF16)
        u_pool = _dot(hb, win_ref[:, 0:POOL_WIDTH])
        u_ssm = _dot(hb, win_ref[:, POOL_WIDTH:POOL_WIDTH + SSM_WIDTH])
        gl_pool = _dot(hb, win_ref[:, 1024:2048])
        gl_ssm = _dot(hb, win_ref[:, 2048:3072])
        gls_ref[pl.ds(r0, rs), :] = jax.nn.sigmoid(gl_ssm)
        us_ref[pl.ds(r0, rs), :] = u_ssm
        h0 = pl.multiple_of(halo + r0, SUBLANES)
        hist_ref[pl.ds(h0, rs), :] = u_pool

        frame = step * (rows // SUBLANES) + k * (rs // SUBLANES) + lax.shift_right_logical(
            lax.broadcasted_iota(jnp.int32, (rs, 1), 0), int(math.log2(SUBLANES)))
        zs = []
        for gi, w in enumerate(POOL_WINDOWS):
            lo = gi * POOL_GROUP
            cur = u_pool[:, lo:lo + POOL_GROUP]
            tot = cur
            for j in range(1, w):
                hj = pl.multiple_of(halo + r0 - j * SUBLANES, SUBLANES)
                tot = tot + hist_ref[pl.ds(hj, rs), lo:lo + POOL_GROUP]
            cnt = jnp.minimum(frame + 1, w).astype(F32)
            z = (tot / cnt - cur).astype(BF16)
            pz = _dot(z, pw_ref[gi]) + pb_ref[:, lo:lo + POOL_GROUP]
            zs.append(pz * ps_ref[:, lo:lo + POOL_GROUP])
        pooled = jnp.concatenate(zs, axis=-1).astype(BF16)
        y_pool = _dot(pooled, wpu_ref[...])
        merged_ref[pl.ds(r0, rs), :] = jax.nn.sigmoid(gl_pool) * y_pool

        ub = u_ssm.astype(BF16)
        kw = btre_ref.shape[1]
        for j in range(btre_ref.shape[0]):
            k0 = (j * MXU_DIM // (SSM_STATE // SSM_GROUP)) // kw * kw
            uk = ub[:, k0:k0 + kw]
            bu_ref[pl.ds(r0, rs), j * MXU_DIM:(j + 1) * MXU_DIM] = _dot(uk, btre_ref[j])
            bu_ref[pl.ds(r0, rs), N_STATES + j * MXU_DIM:N_STATES + (j + 1) * MXU_DIM] = \
                _dot(uk, btim_ref[j])
        return carry

    lax.fori_loop(0, rows // rs, phase1, 0)
    hist_ref[0:halo, :] = hist_ref[rows:rows + halo, :]

    for c in range(N_STATES // SCAN_LANES):
        re = slice(c * SCAN_LANES, (c + 1) * SCAN_LANES)
        im = slice(N_STATES + c * SCAN_LANES, N_STATES + (c + 1) * SCAN_LANES)
        a_re = are_ref[:, re]
        a_im = aim_ref[:, re]

        def frame_step(t, s, re=re, im=im, a_re=a_re, a_im=a_im):
            s_re, s_im = s
            r0 = pl.multiple_of(t * SUBLANES, SUBLANES)
            n_re = a_re * s_re - a_im * s_im + bu_ref[pl.ds(r0, SUBLANES), re]
            n_im = a_re * s_im + a_im * s_re + bu_ref[pl.ds(r0, SUBLANES), im]
            bu_ref[pl.ds(r0, SUBLANES), re] = n_re
            bu_ref[pl.ds(r0, SUBLANES), im] = n_im
            return n_re, n_im

        s_re, s_im = lax.fori_loop(0, rows // SUBLANES, frame_step,
                                   (state_ref[:, re], state_ref[:, im]), unroll=SCAN_UNROLL)
        state_ref[:, re] = s_re
        state_ref[:, im] = s_im

    def phase3(k, carry):
        r0 = pl.multiple_of(k * rs, rs)
        kc = ctre_ref.shape[1]
        ys = []
        for j in range(ctre_ref.shape[0]):
            s_re = bu_ref[pl.ds(r0, rs), j * kc:(j + 1) * kc].astype(BF16)
            s_im = bu_ref[pl.ds(r0, rs), N_STATES + j * kc:N_STATES + (j + 1) * kc].astype(BF16)
            ys.append(_dot(s_re, ctre_ref[j]) - _dot(s_im, ctim_ref[j]))
        y = jnp.concatenate(ys, axis=-1) + dsk_ref[...] * us_ref[pl.ds(r0, rs), :]
        y = 0.5 * y * (1.0 + lax.erf(y * (1.0 / math.sqrt(2.0))))
        vg = _dot(y.astype(BF16), wglu_ref[...]) + bglu_ref[...]
        glu = vg[:, 0:SSM_WIDTH] * jax.nn.sigmoid(vg[:, SSM_WIDTH:2 * SSM_WIDTH])
        y_ssm = _dot(glu.astype(BF16), wsu_ref[...])
        merged = merged_ref[pl.ds(r0, rs), :] + gls_ref[pl.ds(r0, rs), :] * y_ssm
        mix = _dot(merged.astype(BF16), wout_ref[...])
        o_ref[pl.ds(r0, rs), :] = x_ref[pl.ds(r0, rs), :] + mod_ref[2] * mix
        return carry

    lax.fori_loop(0, rows // rs, phase3, 0)


def _block_diag(blocks):
    g, k, n = blocks.shape
    eye = jnp.eye(g, dtype=blocks.dtype)
    return jnp.einsum("gkn,gh->gkhn", blocks, eye).reshape(g * k, g * n)


def _mixer(x_tm, mod_rows, g_mix, w_in, pool_w, pool_b, pool_scale, w_pool_up,
           ab_re, ab_im, bb_re, bb_im, c_re, c_im, d_skip, w_glu, b_glu, w_ssm_up, w_out,
           *, bsz, seq):
    d, tt, rs = D_MODEL, MIX_TT, MIX_RS
    rows = tt * bsz
    g, n, h = N_SSM_GROUPS, SSM_STATE, SSM_GROUP

    kw = LANES
    ntile = N_STATES // MXU_DIM

    def b_tiles(bb):
        dense = _block_diag(bb.reshape(g, n, h).transpose(0, 2, 1)).astype(BF16)
        tiles = []
        for j in range(ntile):
            k0 = (j * MXU_DIM // (n // h)) // kw * kw
            tiles.append(dense[k0:k0 + kw, j * MXU_DIM:(j + 1) * MXU_DIM])
        return jnp.stack(tiles)

    def c_tiles(cc):
        dense = _block_diag(cc.transpose(0, 2, 1)).astype(BF16)
        nt = SSM_WIDTH // MXU_DIM
        kc = N_STATES // nt
        return jnp.stack([dense[j * kc:(j + 1) * kc, j * MXU_DIM:(j + 1) * MXU_DIM]
                          for j in range(nt)])

    lane = lambda a: jnp.broadcast_to(a.reshape(1, N_STATES), (bsz, N_STATES))
    mod_tiled = jnp.tile(mod_rows, (1, rs // bsz, 1))
    x2 = x_tm.reshape(seq * bsz, d)
    args = (
        x2, mod_tiled, g_mix.reshape(1, d), w_in.astype(BF16), pool_w.astype(BF16),
        pool_b.reshape(1, POOL_WIDTH), pool_scale.reshape(1, POOL_WIDTH), w_pool_up.astype(BF16),
        b_tiles(bb_re), b_tiles(bb_im), lane(ab_re), lane(ab_im),
        c_tiles(c_re), c_tiles(c_im), d_skip.reshape(1, SSM_WIDTH),
        w_glu.astype(BF16), b_glu.reshape(1, 2 * SSM_WIDTH), w_ssm_up.astype(BF16),
        w_out.astype(BF16),
    )
    row_spec = pl.BlockSpec((rows, d), lambda i: (i, 0))
    in_specs = [row_spec] + [_resident(a.shape) for a in args[1:]]
    out = pl.pallas_call(
        _mixer_kernel,
        grid=(seq // tt,),
        in_specs=in_specs,
        out_specs=row_spec,
        out_shape=jax.ShapeDtypeStruct((seq * bsz, d), F32),
        scratch_shapes=[
            pltpu.VMEM((rows + HALO_FRAMES * SUBLANES, POOL_WIDTH), F32),
            pltpu.VMEM((bsz, 2 * N_STATES), F32),
            pltpu.VMEM((rows, 2 * N_STATES), F32),
            pltpu.VMEM((rows, d), F32),
            pltpu.VMEM((rows, d), F32),
            pltpu.VMEM((rows, SSM_WIDTH), F32),
        ],
        compiler_params=pltpu.CompilerParams(
            dimension_semantics=("arbitrary",), vmem_limit_bytes=VMEM_LIMIT_BYTES),
        name="mixer",
    )(*args)
    return out.reshape(seq, bsz * d)


def kernel(x, c, w_ada, b_ada, g_ffn1, w_ffn1_in, w_ffn1_out, g_mix, w_in, pool_w, pool_b,
           pool_scale, w_pool_up, ssm_lam_re_log, ssm_lam_im, ssm_log_dt, ssm_b_re, ssm_b_im,
           ssm_c_re, ssm_c_im, ssm_d, w_glu, b_glu, w_ssm_up, w_out, g_ffn2, w_ffn2_in,
           w_ffn2_out, g_final):
    bsz, seq, d = x.shape
    assert (bsz, d) == (SUBLANES, D_MODEL) and seq % FFN_TM == 0 and seq % MIX_TT == 0
    assert w_ada.shape[0] == 1, "single-layer block"
    l = 0
    mod = _adaln(c, w_ada[l], b_ada[l]).reshape(bsz, 9, d)
    ab_re, ab_im, bb_re, bb_im = _ssm_prep(ssm_lam_re_log[l], ssm_lam_im[l], ssm_log_dt[l],
                                           ssm_b_re[l], ssm_b_im[l])
    x1 = _ffn(x, mod, g_ffn1[l], w_ffn1_in[l], w_ffn1_out[l], g_final,
              sublayer=0, time_major_in=False, final_norm=False, bsz=bsz, seq=seq)
    mod_mix = mod[:, 3:6, :].transpose(1, 0, 2)
    x2 = _mixer(x1, mod_mix, g_mix[l], w_in[l], pool_w[l], pool_b[l], pool_scale[l],
                w_pool_up[l], ab_re, ab_im, bb_re, bb_im, ssm_c_re[l], ssm_c_im[l], ssm_d[l],
                w_glu[l], b_glu[l], w_ssm_up[l], w_out[l], bsz=bsz, seq=seq)
    return _ffn(x2, mod, g_ffn2[l], w_ffn2_in[l], w_ffn2_out[l], g_final,
                sublayer=2, time_major_in=True, final_norm=True, bsz=bsz, seq=seq)
```

```python
import functools
import math

import jax
import jax.numpy as jnp
from jax import lax
from jax.experimental import pallas as pl
from jax.experimental.pallas import tpu as pltpu

D_MODEL = 1024
D_FF = 2816
N_MOD_ROWS = 9
POOL_WIDTH = 512
POOL_WINDOWS = (2, 4, 8, 16)
POOL_GROUP = 128
SSM_WIDTH = 512
SSM_GROUP = 16
N_SSM_GROUPS = 32
SSM_STATE = 64
N_STATES = N_SSM_GROUPS * SSM_STATE
EPS = 1e-6

F32 = jnp.float32
BF16 = jnp.bfloat16

SUBLANES = 8
LANES = 128
MXU_DIM = 256
VMEM_LIMIT_BYTES = 60000 * 1024

N_SLABS = D_MODEL // LANES
TILE_FRAMES = 64
FFN_FC = MXU_DIM
FFN_PARTS = 2
HALO_FRAMES = 16
SCAN_LANES = 512
B_TILE_K = LANES


def _resident(shape):
    nd = len(shape)
    return pl.BlockSpec(shape, lambda *_: (0,) * nd, pipeline_mode=pl.Buffered(1))


def _rms_scale(x):
    return x * lax.rsqrt(jnp.mean(x * x, axis=-1, keepdims=True) + EPS)


def _dot(a, b):
    return jnp.dot(a, b, preferred_element_type=F32)


def _adaln_kernel(c_ref, w_ref, b_ref, o_ref):
    c = c_ref[...]
    s = c * jax.nn.sigmoid(c)
    o_ref[...] = _dot(s.astype(BF16), w_ref[...].astype(BF16)) + b_ref[...]


def _adaln(c, w, b):
    bsz, d = c.shape
    n = w.shape[1]
    tn = n // 8
    return pl.pallas_call(
        _adaln_kernel,
        grid=(n // tn,),
        in_specs=[
            pl.BlockSpec((bsz, d), lambda j: (0, 0)),
            pl.BlockSpec((d, tn), lambda j: (0, j)),
            pl.BlockSpec((1, tn), lambda j: (0, j)),
        ],
        out_specs=pl.BlockSpec((bsz, tn), lambda j: (0, j)),
        out_shape=jax.ShapeDtypeStruct((bsz, n), F32),
        compiler_params=pltpu.CompilerParams(dimension_semantics=("arbitrary",)),
        name="adaln",
    )(c, w, b.reshape(1, n))


def _ssm_prep_kernel(lre_ref, lim_ref, ldt_ref, bre_ref, bim_ref,
                     are_ref, aim_ref, bbre_ref, bbim_ref):
    lr = -jnp.exp(lre_ref[...])
    li = lim_ref[...]
    dt = jnp.exp(ldt_ref[...])
    mag = jnp.exp(lr * dt)
    ang = li * dt
    ab_re = mag * jnp.cos(ang)
    ab_im = mag * jnp.sin(ang)
    num_re = ab_re - 1.0
    num_im = ab_im
    den = lr * lr + li * li
    f_re = (num_re * lr + num_im * li) / den
    f_im = (num_im * lr - num_re * li) / den
    br = bre_ref[...]
    bi = bim_ref[...]
    are_ref[...] = ab_re
    aim_ref[...] = ab_im
    bbre_ref[...] = f_re * br - f_im * bi
    bbim_ref[...] = f_re * bi + f_im * br


def _ssm_prep(lam_re_log, lam_im, log_dt, b_re, b_im):
    g, n, h = b_re.shape
    rows = g * n
    col = lambda a: a.reshape(rows, 1)
    ldt = jnp.broadcast_to(log_dt[:, None], (g, n))
    full = lambda shape: pl.BlockSpec(shape, lambda: (0,) * len(shape))
    return pl.pallas_call(
        _ssm_prep_kernel,
        in_specs=[full((rows, 1))] * 3 + [full((rows, h))] * 2,
        out_specs=[full((rows, 1))] * 2 + [full((rows, h))] * 2,
        out_shape=[jax.ShapeDtypeStruct((rows, 1), F32)] * 2
        + [jax.ShapeDtypeStruct((rows, h), F32)] * 2,
        name="ssm_prep",
    )(col(lam_re_log), col(lam_im), col(ldt), b_re.reshape(rows, h), b_im.reshape(rows, h))


def _ffn_kernel(x_ref, mod_ref, g_ref, win_ref, wout_ref, gfin_ref, o_ref, *,
                sublayer, slab_in, final_norm):
    bsz, tt, fc = SUBLANES, TILE_FRAMES, FFN_FC
    nc = D_FF // fc
    per = bsz // FFN_PARTS
    row = lambda b, k: mod_ref[b, 3 * sublayer + k:3 * sublayer + k + 1, :]

    def load_x(b, lo, hi):
        if slab_in:
            return jnp.concatenate(
                [x_ref[s, pl.ds(b, tt, stride=bsz), :] for s in range(lo // LANES, hi // LANES)],
                axis=-1)
        return x_ref[b, :, lo:hi]

    for p in range(FFN_PARTS):
        seqs = range(p * per, (p + 1) * per)
        hs = []
        for b in seqs:
            gs = g_ref[...] * (1.0 + row(b, 1))
            hs.append((_rms_scale(load_x(b, 0, D_MODEL)) * gs + row(b, 0)).astype(BF16))
        hb = jnp.concatenate(hs, axis=0)
        gl = []
        for c in range(nc):
            a = _dot(hb, win_ref[:, c * fc:(c + 1) * fc])
            bb = _dot(hb, win_ref[:, D_FF + c * fc:D_FF + (c + 1) * fc])
            gl.append((a * jax.nn.sigmoid(a) * bb).astype(BF16))
        glu = jnp.concatenate(gl, axis=-1)
        res = [_dot(glu, wout_ref[:, n * fc:(n + 1) * fc]) for n in range(D_MODEL // fc)]
        for bi, b in enumerate(seqs):
            rows = slice(bi * tt, (bi + 1) * tt)
            if slab_in:
                delta = jnp.concatenate([r[rows] for r in res], axis=-1)
                out = load_x(b, 0, D_MODEL) + (0.5 * row(b, 2)) * delta
                if final_norm:
                    out = _rms_scale(out) * gfin_ref[...]
                o_ref[b] = out
            else:
                assert not final_norm
                for n, r in enumerate(res):
                    lo = n * fc
                    out = load_x(b, lo, lo + fc) + (0.5 * row(b, 2)[:, lo:lo + fc]) * r[rows]
                    for s in range(fc // LANES):
                        o_ref[lo // LANES + s, pl.ds(b, tt, stride=bsz), :] = \
                            out[:, s * LANES:(s + 1) * LANES]


def _ffn(x, mod, g, w_in, w_out, g_final, *, sublayer, slab_in, final_norm, bsz, seq):
    d, f, tt = D_MODEL, D_FF, TILE_FRAMES
    nat_spec = pl.BlockSpec((bsz, tt, d), lambda i: (0, i, 0))
    slab_spec = pl.BlockSpec((N_SLABS, tt * bsz, LANES), lambda i: (0, i, 0))
    if slab_in:
        in_spec, out_spec = slab_spec, nat_spec
        out_shape = jax.ShapeDtypeStruct((bsz, seq, d), F32)
    else:
        in_spec, out_spec = nat_spec, slab_spec
        out_shape = jax.ShapeDtypeStruct((N_SLABS, seq * bsz, LANES), F32)
    return pl.pallas_call(
        functools.partial(_ffn_kernel, sublayer=sublayer, slab_in=slab_in, final_norm=final_norm),
        grid=(seq // tt,),
        in_specs=[
            in_spec,
            _resident((bsz, N_MOD_ROWS, d)),
            _resident((1, d)),
            _resident((d, 2 * f)),
            _resident((f, d)),
            _resident((1, d)),
        ],
        out_specs=out_spec,
        out_shape=out_shape,
        compiler_params=pltpu.CompilerParams(
            dimension_semantics=("arbitrary",), vmem_limit_bytes=VMEM_LIMIT_BYTES),
        name=f"ffn{sublayer}",
    )(x, mod, g.reshape(1, d), w_in.astype(BF16), w_out.astype(BF16), g_final.reshape(1, d))


def _mixer_kernel(x_ref, mod_ref, g_ref, win_ref, pw_ref, pb_ref, ps_ref, wpu_ref,
                  btre_ref, btim_ref, are_ref, aim_ref, ctre_ref, ctim_ref, dsk_ref,
                  wglu_ref, bglu_ref, wsu_ref, wout_ref, o_ref,
                  hist_ref, state_ref, bu_ref):
    step = pl.program_id(0)
    bsz, tt, d = SUBLANES, TILE_FRAMES, D_MODEL
    rows = tt * bsz
    halo = HALO_FRAMES * bsz
    tile3 = lambda v: v.reshape(tt, bsz, v.shape[-1])
    flat = lambda v: v.reshape(rows, v.shape[-1])

    @pl.when(step == 0)
    def _():
        state_ref[...] = jnp.zeros_like(state_ref)
        hist_ref[0:halo, :] = jnp.zeros((halo, POOL_WIDTH), F32)

    x = jnp.concatenate([x_ref[s] for s in range(N_SLABS)], axis=-1)
    gs = g_ref[...] * (1.0 + mod_ref[1])
    hb = flat(tile3(_rms_scale(x)) * gs[None] + mod_ref[0][None]).astype(BF16)
    u_pool = _dot(hb, win_ref[:, 0:POOL_WIDTH])
    u_ssm = _dot(hb, win_ref[:, POOL_WIDTH:POOL_WIDTH + SSM_WIDTH])
    g0 = POOL_WIDTH + SSM_WIDTH

    hist_ref[halo:halo + rows, :] = u_pool
    frame = step * tt + lax.shift_right_logical(
        lax.broadcasted_iota(jnp.int32, (rows, 1), 0), int(math.log2(bsz)))
    zs = []
    for gi, w in enumerate(POOL_WINDOWS):
        lo = gi * POOL_GROUP
        cur = u_pool[:, lo:lo + POOL_GROUP]
        tot = cur
        for j in range(1, w):
            tot = tot + hist_ref[halo - j * bsz:halo - j * bsz + rows, lo:lo + POOL_GROUP]
        cnt = jnp.minimum(frame + 1, w).astype(F32)
        z = (tot / cnt - cur).astype(BF16)
        pz = _dot(z, pw_ref[gi]) + pb_ref[:, lo:lo + POOL_GROUP]
        zs.append(pz * ps_ref[:, lo:lo + POOL_GROUP])
    hist_ref[0:halo, :] = hist_ref[rows:rows + halo, :]
    pooled = jnp.concatenate(zs, axis=-1).astype(BF16)
    merged = jax.nn.sigmoid(_dot(hb, win_ref[:, g0:g0 + d])) * _dot(pooled, wpu_ref[...])
    gl_ssm = jax.nn.sigmoid(_dot(hb, win_ref[:, g0 + d:g0 + 2 * d]))

    ub = u_ssm.astype(BF16)
    kw = btre_ref.shape[1]
    for j in range(btre_ref.shape[0]):
        k0 = (j * MXU_DIM // (SSM_STATE // SSM_GROUP)) // kw * kw
        uk = ub[:, k0:k0 + kw]
        bu_ref[:, j * MXU_DIM:(j + 1) * MXU_DIM] = _dot(uk, btre_ref[j])
        bu_ref[:, N_STATES + j * MXU_DIM:N_STATES + (j + 1) * MXU_DIM] = _dot(uk, btim_ref[j])

    for c in range(N_STATES // SCAN_LANES):
        re = slice(c * SCAN_LANES, (c + 1) * SCAN_LANES)
        im = slice(N_STATES + c * SCAN_LANES, N_STATES + (c + 1) * SCAN_LANES)
        a_re, a_im = are_ref[:, re], aim_ref[:, re]
        s_re, s_im = state_ref[:, re], state_ref[:, im]
        for t in range(tt):
            fr = slice(t * bsz, (t + 1) * bsz)
            n_re = a_re * s_re - a_im * s_im + bu_ref[fr, re]
            n_im = a_re * s_im + a_im * s_re + bu_ref[fr, im]
            bu_ref[fr, re] = n_re
            bu_ref[fr, im] = n_im
            s_re, s_im = n_re, n_im
        state_ref[:, re] = s_re
        state_ref[:, im] = s_im

    kc = ctre_ref.shape[1]
    ys = []
    for j in range(ctre_ref.shape[0]):
        s_re = bu_ref[:, j * kc:(j + 1) * kc].astype(BF16)
        s_im = bu_ref[:, N_STATES + j * kc:N_STATES + (j + 1) * kc].astype(BF16)
        ys.append(_dot(s_re, ctre_ref[j]) - _dot(s_im, ctim_ref[j]))
    y = jnp.concatenate(ys, axis=-1) + dsk_ref[...] * u_ssm
    y = 0.5 * y * (1.0 + lax.erf(y * (1.0 / math.sqrt(2.0))))
    vg = _dot(y.astype(BF16), wglu_ref[...]) + bglu_ref[...]
    glu = vg[:, 0:SSM_WIDTH] * jax.nn.sigmoid(vg[:, SSM_WIDTH:2 * SSM_WIDTH])
    merged = merged + gl_ssm * _dot(glu.astype(BF16), wsu_ref[...])
    mix = _dot(merged.astype(BF16), wout_ref[...])
    out = x + flat(mod_ref[2][None] * tile3(mix))
    for s in range(N_SLABS):
        o_ref[s] = out[:, s * LANES:(s + 1) * LANES]


def _drive_tiles(bb):
    g, n, h = N_SSM_GROUPS, SSM_STATE, SSM_GROUP
    gpt = MXU_DIM // n
    gpw = B_TILE_K // h
    tpw = gpw // gpt
    ntile = g // gpt
    bb5 = bb.reshape(ntile // tpw, tpw, gpt, n, h).transpose(0, 1, 2, 4, 3)
    place = (jnp.arange(gpw)[None, :, None]
             == gpt * jnp.arange(tpw)[:, None, None] + jnp.arange(gpt)[None, None, :])
    tiles = jnp.einsum("jpqhn,prq->jprhqn", bb5, place.astype(bb.dtype))
    return tiles.reshape(ntile, B_TILE_K, MXU_DIM).astype(BF16)


def _readout_tiles(cc):
    g, h, n = cc.shape
    nt = SSM_WIDTH // MXU_DIM
    gpt = g // nt
    eye = jnp.eye(gpt, dtype=cc.dtype)
    tiles = jnp.einsum("jghn,gk->jgnkh", cc.reshape(nt, gpt, h, n), eye)
    return tiles.reshape(nt, gpt * n, MXU_DIM).astype(BF16)


def _mixer(x_slab, mod_rows, g_mix, w_in, pool_w, pool_b, pool_scale, w_pool_up,
           ab_re, ab_im, bb_re, bb_im, c_re, c_im, d_skip, w_glu, b_glu, w_ssm_up, w_out,
           *, bsz, seq):
    d, tt = D_MODEL, TILE_FRAMES
    rows = tt * bsz
    lane = lambda a: jnp.broadcast_to(a.reshape(1, N_STATES), (bsz, N_STATES))
    args = (
        x_slab, mod_rows, g_mix.reshape(1, d), w_in.astype(BF16), pool_w.astype(BF16),
        pool_b.reshape(1, POOL_WIDTH), pool_scale.reshape(1, POOL_WIDTH), w_pool_up.astype(BF16),
        _drive_tiles(bb_re), _drive_tiles(bb_im), lane(ab_re), lane(ab_im),
        _readout_tiles(c_re), _readout_tiles(c_im), d_skip.reshape(1, SSM_WIDTH),
        w_glu.astype(BF16), b_glu.reshape(1, 2 * SSM_WIDTH), w_ssm_up.astype(BF16),
        w_out.astype(BF16),
    )
    slab_spec = pl.BlockSpec((N_SLABS, rows, LANES), lambda i: (0, i, 0))
    return pl.pallas_call(
        _mixer_kernel,
        grid=(seq // tt,),
        in_specs=[slab_spec] + [_resident(a.shape) for a in args[1:]],
        out_specs=slab_spec,
        out_shape=jax.ShapeDtypeStruct((N_SLABS, seq * bsz, LANES), F32),
        scratch_shapes=[
            pltpu.VMEM((rows + HALO_FRAMES * bsz, POOL_WIDTH), F32),
            pltpu.VMEM((bsz, 2 * N_STATES), F32),
            pltpu.VMEM((rows, 2 * N_STATES), F32),
        ],
        compiler_params=pltpu.CompilerParams(
            dimension_semantics=("arbitrary",), vmem_limit_bytes=VMEM_LIMIT_BYTES),
        name="mixer",
    )(*args)


def kernel(x, c, w_ada, b_ada, g_ffn1, w_ffn1_in, w_ffn1_out, g_mix, w_in, pool_w, pool_b,
           pool_scale, w_pool_up, ssm_lam_re_log, ssm_lam_im, ssm_log_dt, ssm_b_re, ssm_b_im,
           ssm_c_re, ssm_c_im, ssm_d, w_glu, b_glu, w_ssm_up, w_out, g_ffn2, w_ffn2_in,
           w_ffn2_out, g_final):
    bsz, seq, d = x.shape
    assert (bsz, d) == (SUBLANES, D_MODEL) and seq % TILE_FRAMES == 0
    assert w_ada.shape[0] == 1, "single-layer block"
    l = 0
    mod = _adaln(c, w_ada[l], b_ada[l]).reshape(bsz, N_MOD_ROWS, d)
    ab_re, ab_im, bb_re, bb_im = _ssm_prep(ssm_lam_re_log[l], ssm_lam_im[l], ssm_log_dt[l],
                                           ssm_b_re[l], ssm_b_im[l])
    x1 = _ffn(x, mod, g_ffn1[l], w_ffn1_in[l], w_ffn1_out[l], g_final,
              sublayer=0, slab_in=False, final_norm=False, bsz=bsz, seq=seq)
    mod_mix = mod[:, 3:6, :].transpose(1, 0, 2)
    x2 = _mixer(x1, mod_mix, g_mix[l], w_in[l], pool_w[l], pool_b[l], pool_scale[l],
                w_pool_up[l], ab_re, ab_im, bb_re, bb_im, ssm_c_re[l], ssm_c_im[l], ssm_d[l],
                w_glu[l], b_glu[l], w_ssm_up[l], w_out[l], bsz=bsz, seq=seq)
    return _ffn(x2, mod, g_ffn2[l], w_ffn2_in[l], w_ffn2_out[l], g_final,
                sublayer=2, slab_in=True, final_norm=True, bsz=bsz, seq=seq)
```

```python
import functools
import math

import jax
import jax.numpy as jnp
from jax import lax
from jax.experimental import pallas as pl
from jax.experimental.pallas import tpu as pltpu

D_MODEL = 1024
D_FF = 2816
N_MOD_ROWS = 9
MIX_SUBLAYER = 1
POOL_WIDTH = 512
POOL_WINDOWS = (2, 4, 8, 16)
POOL_GROUP = 128
SSM_WIDTH = 512
SSM_GROUP = 16
N_SSM_GROUPS = 32
SSM_STATE = 64
N_STATES = N_SSM_GROUPS * SSM_STATE
EPS = 1e-6

F32 = jnp.float32
BF16 = jnp.bfloat16

SUBLANES = 8
LANES = 128
MXU_DIM = 256
VMEM_LIMIT_BYTES = 60000 * 1024

N_SLABS = D_MODEL // LANES
MIX_FRAMES = 64
FFN_FRAMES = 128
FFN_FC = MXU_DIM
FFN_PARTS = 4
HALO_FRAMES = 16
SCAN_LANES = 512
B_TILE_K = LANES


def _resident(shape):
    nd = len(shape)
    return pl.BlockSpec(shape, lambda *_: (0,) * nd, pipeline_mode=pl.Buffered(1))


def _rms_scale(x):
    return x * lax.rsqrt(jnp.mean(x * x, axis=-1, keepdims=True) + EPS)


def _dot(a, b):
    return jnp.dot(a, b, preferred_element_type=F32)


def _adaln_kernel(c_ref, w_ref, b_ref, o_ref):
    c = c_ref[...]
    s = c * jax.nn.sigmoid(c)
    o_ref[...] = _dot(s.astype(BF16), w_ref[...].astype(BF16)) + b_ref[...]


def _adaln(c, w, b):
    bsz, d = c.shape
    n = w.shape[1]
    return pl.pallas_call(
        _adaln_kernel,
        grid=(n // d,),
        in_specs=[
            pl.BlockSpec((bsz, d), lambda j: (0, 0)),
            pl.BlockSpec((d, d), lambda j: (0, j)),
            pl.BlockSpec((1, d), lambda j: (0, j)),
        ],
        out_specs=pl.BlockSpec((None, bsz, d), lambda j: (j, 0, 0)),
        out_shape=jax.ShapeDtypeStruct((n // d, bsz, d), F32),
        compiler_params=pltpu.CompilerParams(dimension_semantics=("arbitrary",)),
        name="adaln",
    )(c, w, b.reshape(1, n))


def _ssm_prep_kernel(lre_ref, lim_ref, ldt_ref, bre_ref, bim_ref,
                     are_ref, aim_ref, bbre_ref, bbim_ref):
    lr = -jnp.exp(lre_ref[...])
    li = lim_ref[...]
    dt = jnp.exp(ldt_ref[...])
    mag = jnp.exp(lr * dt)
    ang = li * dt
    ab_re = mag * jnp.cos(ang)
    ab_im = mag * jnp.sin(ang)
    num_re = ab_re - 1.0
    num_im = ab_im
    den = lr * lr + li * li
    f_re = (num_re * lr + num_im * li) / den
    f_im = (num_im * lr - num_re * li) / den
    br = bre_ref[...]
    bi = bim_ref[...]
    are_ref[...] = ab_re
    aim_ref[...] = ab_im
    bbre_ref[...] = f_re * br - f_im * bi
    bbim_ref[...] = f_re * bi + f_im * br


def _ssm_prep(lam_re_log, lam_im, log_dt, b_re, b_im):
    g, n, h = b_re.shape
    rows = g * n
    col = lambda a: a.reshape(rows, 1)
    ldt = jnp.broadcast_to(log_dt[:, None], (g, n))
    full = lambda shape: pl.BlockSpec(shape, lambda: (0,) * len(shape))
    return pl.pallas_call(
        _ssm_prep_kernel,
        in_specs=[full((rows, 1))] * 3 + [full((rows, h))] * 2,
        out_specs=[full((rows, 1))] * 2 + [full((rows, h))] * 2,
        out_shape=[jax.ShapeDtypeStruct((rows, 1), F32)] * 2
        + [jax.ShapeDtypeStruct((rows, h), F32)] * 2,
        name="ssm_prep",
    )(col(lam_re_log), col(lam_im), col(ldt), b_re.reshape(rows, h), b_im.reshape(rows, h))


def _ffn_kernel(x_ref, mod_ref, g_ref, win_ref, wout_ref, gfin_ref, o_ref, *,
                sublayer, slab_in, final_norm):
    bsz, tt, fc = SUBLANES, FFN_FRAMES, FFN_FC
    nc = D_FF // fc
    per = bsz // FFN_PARTS
    row = lambda b, k: mod_ref[3 * sublayer + k, b:b + 1, :]

    def load_x(b, lo, hi):
        if slab_in:
            return jnp.concatenate(
                [x_ref[s, pl.ds(b, tt, stride=bsz), :] for s in range(lo // LANES, hi // LANES)],
                axis=-1)
        return x_ref[b, :, lo:hi]

    for p in range(FFN_PARTS):
        seqs = range(p * per, (p + 1) * per)
        hs = []
        for b in seqs:
            gs = g_ref[...] * (1.0 + row(b, 1))
            hs.append((_rms_scale(load_x(b, 0, D_MODEL)) * gs + row(b, 0)).astype(BF16))
        hb = jnp.concatenate(hs, axis=0)
        gl = []
        for c in range(nc):
            a = _dot(hb, win_ref[:, c * fc:(c + 1) * fc])
            bb = _dot(hb, win_ref[:, D_FF + c * fc:D_FF + (c + 1) * fc])
            gl.append((a * jax.nn.sigmoid(a) * bb).astype(BF16))
        glu = jnp.concatenate(gl, axis=-1)
        res = [_dot(glu, wout_ref[:, n * fc:(n + 1) * fc]) for n in range(D_MODEL // fc)]
        for bi, b in enumerate(seqs):
            rows = slice(bi * tt, (bi + 1) * tt)
            if slab_in:
                delta = jnp.concatenate([r[rows] for r in res], axis=-1)
                out = load_x(b, 0, D_MODEL) + (0.5 * row(b, 2)) * delta
                if final_norm:
                    out = _rms_scale(out) * gfin_ref[...]
                o_ref[b] = out
            else:
                assert not final_norm
                for n, r in enumerate(res):
                    lo = n * fc
                    out = load_x(b, lo, lo + fc) + (0.5 * row(b, 2)[:, lo:lo + fc]) * r[rows]
                    for s in range(fc // LANES):
                        o_ref[lo // LANES + s, pl.ds(b, tt, stride=bsz), :] = \
                            out[:, s * LANES:(s + 1) * LANES]


def _ffn(x, mod, g, w_in, w_out, g_final, *, sublayer, slab_in, final_norm, bsz, seq):
    d, f, tt = D_MODEL, D_FF, FFN_FRAMES
    nat_spec = pl.BlockSpec((bsz, tt, d), lambda i: (0, i, 0))
    slab_spec = pl.BlockSpec((N_SLABS, tt * bsz, LANES), lambda i: (0, i, 0))
    if slab_in:
        in_spec, out_spec = slab_spec, nat_spec
        out_shape = jax.ShapeDtypeStruct((bsz, seq, d), F32)
    else:
        in_spec, out_spec = nat_spec, slab_spec
        out_shape = jax.ShapeDtypeStruct((N_SLABS, seq * bsz, LANES), F32)
    return pl.pallas_call(
        functools.partial(_ffn_kernel, sublayer=sublayer, slab_in=slab_in, final_norm=final_norm),
        grid=(seq // tt,),
        in_specs=[
            in_spec,
            _resident((N_MOD_ROWS, bsz, d)),
            _resident((1, d)),
            _resident((d, 2 * f)),
            _resident((f, d)),
            _resident((1, d)),
        ],
        out_specs=out_spec,
        out_shape=out_shape,
        compiler_params=pltpu.CompilerParams(
            dimension_semantics=("arbitrary",), vmem_limit_bytes=VMEM_LIMIT_BYTES),
        name=f"ffn{sublayer}",
    )(x, mod, g.reshape(1, d), w_in.astype(BF16), w_out.astype(BF16), g_final.reshape(1, d))


def _mixer_kernel(x_ref, mod_ref, g_ref, win_ref, pw_ref, pb_ref, ps_ref, wpu_ref,
                  btre_ref, btim_ref, are_ref, aim_ref, ctre_ref, ctim_ref, dsk_ref,
                  wglu_ref, bglu_ref, wsu_ref, wout_ref, o_ref,
                  hist_ref, state_ref, bu_ref):
    step = pl.program_id(0)
    bsz, tt, d = SUBLANES, MIX_FRAMES, D_MODEL
    rows = tt * bsz
    halo = HALO_FRAMES * bsz
    shift, scale, gate = (mod_ref[3 * MIX_SUBLAYER + k] for k in range(3))
    tile3 = lambda v: v.reshape(tt, bsz, v.shape[-1])
    flat = lambda v: v.reshape(rows, v.shape[-1])

    @pl.when(step == 0)
    def _():
        state_ref[...] = jnp.zeros_like(state_ref)
        hist_ref[0:halo, :] = jnp.zeros((halo, POOL_WIDTH), F32)

    x = jnp.concatenate([x_ref[s] for s in range(N_SLABS)], axis=-1)
    gs = g_ref[...] * (1.0 + scale)
    hb = flat(tile3(_rms_scale(x)) * gs[None] + shift[None]).astype(BF16)
    u_pool = _dot(hb, win_ref[:, 0:POOL_WIDTH])
    u_ssm = _dot(hb, win_ref[:, POOL_WIDTH:POOL_WIDTH + SSM_WIDTH])
    g0 = POOL_WIDTH + SSM_WIDTH

    hist_ref[halo:halo + rows, :] = u_pool
    frame = step * tt + lax.shift_right_logical(
        lax.broadcasted_iota(jnp.int32, (rows, 1), 0), int(math.log2(bsz)))
    zs = []
    for gi, w in enumerate(POOL_WINDOWS):
        lo = gi * POOL_GROUP
        cur = u_pool[:, lo:lo + POOL_GROUP]
        tot = cur
        for j in range(1, w):
            tot = tot + hist_ref[halo - j * bsz:halo - j * bsz + rows, lo:lo + POOL_GROUP]
        cnt = jnp.minimum(frame + 1, w).astype(F32)
        z = (tot / cnt - cur).astype(BF16)
        pz = _dot(z, pw_ref[gi]) + pb_ref[:, lo:lo + POOL_GROUP]
        zs.append(pz * ps_ref[:, lo:lo + POOL_GROUP])
    hist_ref[0:halo, :] = hist_ref[rows:rows + halo, :]
    pooled = jnp.concatenate(zs, axis=-1).astype(BF16)
    merged = jax.nn.sigmoid(_dot(hb, win_ref[:, g0:g0 + d])) * _dot(pooled, wpu_ref[...])
    gl_ssm = jax.nn.sigmoid(_dot(hb, win_ref[:, g0 + d:g0 + 2 * d]))

    ub = u_ssm.astype(BF16)
    kw = btre_ref.shape[1]
    for j in range(btre_ref.shape[0]):
        k0 = (j * MXU_DIM // (SSM_STATE // SSM_GROUP)) // kw * kw
        uk = ub[:, k0:k0 + kw]
        bu_ref[:, j * MXU_DIM:(j + 1) * MXU_DIM] = _dot(uk, btre_ref[j])
        bu_ref[:, N_STATES + j * MXU_DIM:N_STATES + (j + 1) * MXU_DIM] = _dot(uk, btim_ref[j])

    for c in range(N_STATES // SCAN_LANES):
        re = slice(c * SCAN_LANES, (c + 1) * SCAN_LANES)
        im = slice(N_STATES + c * SCAN_LANES, N_STATES + (c + 1) * SCAN_LANES)
        a_re, a_im = are_ref[:, re], aim_ref[:, re]
        s_re, s_im = state_ref[:, re], state_ref[:, im]
        for t in range(tt):
            fr = slice(t * bsz, (t + 1) * bsz)
            n_re = a_re * s_re - a_im * s_im + bu_ref[fr, re]
            n_im = a_re * s_im + a_im * s_re + bu_ref[fr, im]
            bu_ref[fr, re] = n_re
            bu_ref[fr, im] = n_im
            s_re, s_im = n_re, n_im
        state_ref[:, re] = s_re
        state_ref[:, im] = s_im

    kc = ctre_ref.shape[1]
    ys = []
    for j in range(ctre_ref.shape[0]):
        s_re = bu_ref[:, j * kc:(j + 1) * kc].astype(BF16)
        s_im = bu_ref[:, N_STATES + j * kc:N_STATES + (j + 1) * kc].astype(BF16)
        ys.append(_dot(s_re, ctre_ref[j]) - _dot(s_im, ctim_ref[j]))
    y = jnp.concatenate(ys, axis=-1) + dsk_ref[...] * u_ssm
    y = 0.5 * y * (1.0 + lax.erf(y * (1.0 / math.sqrt(2.0))))
    vg = _dot(y.astype(BF16), wglu_ref[...]) + bglu_ref[...]
    glu = vg[:, 0:SSM_WIDTH] * jax.nn.sigmoid(vg[:, SSM_WIDTH:2 * SSM_WIDTH])
    merged = merged + gl_ssm * _dot(glu.astype(BF16), wsu_ref[...])
    mix = _dot(merged.astype(BF16), wout_ref[...])
    out = x + flat(gate[None] * tile3(mix))
    for s in range(N_SLABS):
        o_ref[s] = out[:, s * LANES:(s + 1) * LANES]


def _drive_tiles(bb):
    g, n, h = N_SSM_GROUPS, SSM_STATE, SSM_GROUP
    gpt = MXU_DIM // n
    gpw = B_TILE_K // h
    tpw = gpw // gpt
    ntile = g // gpt
    bb5 = bb.reshape(ntile // tpw, tpw, gpt, n, h).transpose(0, 1, 2, 4, 3)
    place = (jnp.arange(gpw)[None, :, None]
             == gpt * jnp.arange(tpw)[:, None, None] + jnp.arange(gpt)[None, None, :])
    tiles = jnp.einsum("jpqhn,prq->jprhqn", bb5, place.astype(bb.dtype))
    return tiles.reshape(ntile, B_TILE_K, MXU_DIM).astype(BF16)


def _readout_tiles(cc):
    g, h, n = cc.shape
    nt = SSM_WIDTH // MXU_DIM
    gpt = g // nt
    eye = jnp.eye(gpt, dtype=cc.dtype)
    tiles = jnp.einsum("jghn,gk->jgnkh", cc.reshape(nt, gpt, h, n), eye)
    return tiles.reshape(nt, gpt * n, MXU_DIM).astype(BF16)


def _mixer(x_slab, mod_rows, g_mix, w_in, pool_w, pool_b, pool_scale, w_pool_up,
           ab_re, ab_im, bb_re, bb_im, c_re, c_im, d_skip, w_glu, b_glu, w_ssm_up, w_out,
           *, bsz, seq):
    d, tt = D_MODEL, MIX_FRAMES
    rows = tt * bsz
    lane = lambda a: jnp.broadcast_to(a.reshape(1, N_STATES), (bsz, N_STATES))
    args = (
        x_slab, mod_rows, g_mix.reshape(1, d), w_in.astype(BF16), pool_w.astype(BF16),
        pool_b.reshape(1, POOL_WIDTH), pool_scale.reshape(1, POOL_WIDTH), w_pool_up.astype(BF16),
        _drive_tiles(bb_re), _drive_tiles(bb_im), lane(ab_re), lane(ab_im),
        _readout_tiles(c_re), _readout_tiles(c_im), d_skip.reshape(1, SSM_WIDTH),
        w_glu.astype(BF16), b_glu.reshape(1, 2 * SSM_WIDTH), w_ssm_up.astype(BF16),
        w_out.astype(BF16),
    )
    slab_spec = pl.BlockSpec((N_SLABS, rows, LANES), lambda i: (0, i, 0))
    return pl.pallas_call(
        _mixer_kernel,
        grid=(seq // tt,),
        in_specs=[slab_spec] + [_resident(a.shape) for a in args[1:]],
        out_specs=slab_spec,
        out_shape=jax.ShapeDtypeStruct((N_SLABS, seq * bsz, LANES), F32),
        scratch_shapes=[
            pltpu.VMEM((rows + HALO_FRAMES * bsz, POOL_WIDTH), F32),
            pltpu.VMEM((bsz, 2 * N_STATES), F32),
            pltpu.VMEM((rows, 2 * N_STATES), F32),
        ],
        compiler_params=pltpu.CompilerParams(
            dimension_semantics=("arbitrary",), vmem_limit_bytes=VMEM_LIMIT_BYTES),
        name="mixer",
    )(*args)


def kernel(x, c, w_ada, b_ada, g_ffn1, w_ffn1_in, w_ffn1_out, g_mix, w_in, pool_w, pool_b,
           pool_scale, w_pool_up, ssm_lam_re_log, ssm_lam_im, ssm_log_dt, ssm_b_re, ssm_b_im,
           ssm_c_re, ssm_c_im, ssm_d, w_glu, b_glu, w_ssm_up, w_out, g_ffn2, w_ffn2_in,
           w_ffn2_out, g_final):
    bsz, seq, d = x.shape
    assert (bsz, d) == (SUBLANES, D_MODEL) and seq % FFN_FRAMES == 0 and seq % MIX_FRAMES == 0
    assert w_ada.shape[0] == 1 and w_ada.shape[2] == N_MOD_ROWS * d, "single-layer block"
    l = 0
    mod = _adaln(c, w_ada[l], b_ada[l])
    ab_re, ab_im, bb_re, bb_im = _ssm_prep(ssm_lam_re_log[l], ssm_lam_im[l], ssm_log_dt[l],
                                           ssm_b_re[l], ssm_b_im[l])
    x1 = _ffn(x, mod, g_ffn1[l], w_ffn1_in[l], w_ffn1_out[l], g_final,
              sublayer=0, slab_in=False, final_norm=False, bsz=bsz, seq=seq)
    x2 = _mixer(x1, mod, g_mix[l], w_in[l], pool_w[l], pool_b[l], pool_scale[l],
                w_pool_up[l], ab_re, ab_im, bb_re, bb_im, ssm_c_re[l], ssm_c_im[l], ssm_d[l],
                w_glu[l], b_glu[l], w_ssm_up[l], w_out[l], bsz=bsz, seq=seq)
    return _ffn(x2, mod, g_ffn2[l], w_ffn2_in[l], w_ffn2_out[l], g_final,
                sublayer=2, slab_in=True, final_norm=True, bsz=bsz, seq=seq)
```

```python
import functools
import math

import jax
import jax.numpy as jnp
from jax import lax
from jax.experimental import pallas as pl
from jax.experimental.pallas import tpu as pltpu

D_MODEL = 1024
D_FF = 2816
N_MOD_ROWS = 9
MIX_SUBLAYER = 1
POOL_WIDTH = 512
POOL_WINDOWS = (2, 4, 8, 16)
POOL_GROUP = 128
SSM_WIDTH = 512
SSM_GROUP = 16
N_SSM_GROUPS = 32
SSM_STATE = 64
N_STATES = N_SSM_GROUPS * SSM_STATE
EPS = 1e-6

F32 = jnp.float32
BF16 = jnp.bfloat16

SUBLANES = 8
LANES = 128
MXU_DIM = 256
BF16_TILE_ROWS = 16
VMEM_LIMIT_BYTES = 60000 * 1024

N_SLABS = D_MODEL // LANES
TILE_FRAMES = 64
PLANE_PITCH = TILE_FRAMES + SUBLANES
FFN_FC = MXU_DIM
FFN_PARTS = 2
HALO_FRAMES = 16
SCAN_LANES = 512
B_TILE_K = LANES


def _resident(shape):
    nd = len(shape)
    return pl.BlockSpec(shape, lambda *_: (0,) * nd, pipeline_mode=pl.Buffered(1))


def _rms_scale(x):
    return x * lax.rsqrt(jnp.mean(x * x, axis=-1, keepdims=True) + EPS)


def _dot(a, b):
    return jnp.dot(a, b, preferred_element_type=F32)


def _adaln_kernel(c_ref, w_ref, b_ref, o_ref):
    c = c_ref[...]
    s = c * jax.nn.sigmoid(c)
    o_ref[...] = _dot(s.astype(BF16), w_ref[...].astype(BF16)) + b_ref[...]


def _adaln(c, w, b):
    bsz, d = c.shape
    n = w.shape[1]
    return pl.pallas_call(
        _adaln_kernel,
        grid=(n // d,),
        in_specs=[
            pl.BlockSpec((bsz, d), lambda j: (0, 0)),
            pl.BlockSpec((d, d), lambda j: (0, j)),
            pl.BlockSpec((1, d), lambda j: (0, j)),
        ],
        out_specs=pl.BlockSpec((None, bsz, d), lambda j: (j, 0, 0)),
        out_shape=jax.ShapeDtypeStruct((n // d, bsz, d), F32),
        compiler_params=pltpu.CompilerParams(dimension_semantics=("arbitrary",)),
        name="adaln",
    )(c, w, b.reshape(1, n))


def _ssm_prep_kernel(lre_ref, lim_ref, ldt_ref, bre_ref, bim_ref,
                     are_ref, aim_ref, bbre_ref, bbim_ref):
    lr = -jnp.exp(lre_ref[...])
    li = lim_ref[...]
    dt = jnp.exp(ldt_ref[...])
    mag = jnp.exp(lr * dt)
    ang = li * dt
    ab_re = mag * jnp.cos(ang)
    ab_im = mag * jnp.sin(ang)
    num_re = ab_re - 1.0
    num_im = ab_im
    den = lr * lr + li * li
    f_re = (num_re * lr + num_im * li) / den
    f_im = (num_im * lr - num_re * li) / den
    br = bre_ref[...]
    bi = bim_ref[...]
    are_ref[...] = ab_re
    aim_ref[...] = ab_im
    bbre_ref[...] = f_re * br - f_im * bi
    bbim_ref[...] = f_re * bi + f_im * br


def _ssm_prep(lam_re_log, lam_im, log_dt, b_re, b_im):
    g, n, h = b_re.shape
    per_state = lambda a: a.reshape(g, 1, n)
    ldt = jnp.broadcast_to(log_dt[:, None, None], (g, 1, n))
    full = lambda shape: pl.BlockSpec(shape, lambda: (0,) * len(shape))
    return pl.pallas_call(
        _ssm_prep_kernel,
        in_specs=[full((g, 1, n))] * 3 + [full((g, h, n))] * 2,
        out_specs=[full((g, 1, n))] * 2 + [full((g, h, n))] * 2,
        out_shape=[jax.ShapeDtypeStruct((g, 1, n), F32)] * 2
        + [jax.ShapeDtypeStruct((g, h, n), F32)] * 2,
        name="ssm_prep",
    )(per_state(lam_re_log), per_state(lam_im), ldt,
      b_re.transpose(0, 2, 1), b_im.transpose(0, 2, 1))


def _cast_plan(rows, nsteps):
    chunk = BF16_TILE_ROWS
    while rows % chunk or rows // chunk > nsteps:
        chunk += BF16_TILE_ROWS
    return chunk, rows // chunk


def _ffn_kernel(x_ref, mod_ref, g_ref, win_ref, wout_ref, gfin_ref, *rest,
                sublayer, slab_in, final_norm, cast_chunks):
    ncast = len(cast_chunks)
    cast_in, o_ref, cast_out = rest[:ncast], rest[ncast], rest[ncast + 1:]
    bsz, tt, fc = SUBLANES, TILE_FRAMES, FFN_FC
    nc = D_FF // fc
    per = bsz // FFN_PARTS
    row = lambda b, k: mod_ref[3 * sublayer + k, b:b + 1, :]

    step = pl.program_id(0)
    for src, dst, nchunks in zip(cast_in, cast_out, cast_chunks):
        @pl.when(step < nchunks)
        def _(src=src, dst=dst):
            dst[...] = src[...].astype(BF16)

    if not slab_in:
        for s in range(N_SLABS):
            for b in range(bsz):
                o_ref[s, b * PLANE_PITCH + tt:(b + 1) * PLANE_PITCH, :] = \
                    jnp.zeros((PLANE_PITCH - tt, LANES), F32)

    def load_x(b, lo, hi):
        if slab_in:
            return jnp.concatenate(
                [x_ref[s, pl.ds(b, tt, stride=bsz), :] for s in range(lo // LANES, hi // LANES)],
                axis=-1)
        return x_ref[b, :, lo:hi]

    for p in range(FFN_PARTS):
        seqs = range(p * per, (p + 1) * per)
        hs = []
        for b in seqs:
            gs = g_ref[...] * (1.0 + row(b, 1))
            hs.append((_rms_scale(load_x(b, 0, D_MODEL)) * gs + row(b, 0)).astype(BF16))
        hb = jnp.concatenate(hs, axis=0)
        gl = []
        for c in range(nc):
            a = _dot(hb, win_ref[:, c * fc:(c + 1) * fc])
            bb = _dot(hb, win_ref[:, D_FF + c * fc:D_FF + (c + 1) * fc])
            gl.append((a * jax.nn.sigmoid(a) * bb).astype(BF16))
        glu = jnp.concatenate(gl, axis=-1)
        res = [_dot(glu, wout_ref[:, n * fc:(n + 1) * fc]) for n in range(D_MODEL // fc)]
        for bi, b in enumerate(seqs):
            rows = slice(bi * tt, (bi + 1) * tt)
            if slab_in:
                delta = jnp.concatenate([r[rows] for r in res], axis=-1)
                out = load_x(b, 0, D_MODEL) + (0.5 * row(b, 2)) * delta
                if final_norm:
                    out = _rms_scale(out) * gfin_ref[...]
                o_ref[b] = out
            else:
                assert not final_norm
                for n, r in enumerate(res):
                    lo = n * fc
                    out = load_x(b, lo, lo + fc) + (0.5 * row(b, 2)[:, lo:lo + fc]) * r[rows]
                    for s in range(fc // LANES):
                        o_ref[lo // LANES + s, b * PLANE_PITCH:b * PLANE_PITCH + tt, :] = \
                            out[:, s * LANES:(s + 1) * LANES]


def _ffn(x, mod, g, w_in, w_out, g_final, *, sublayer, slab_in, final_norm, bsz, seq, cast=()):
    d, f, tt = D_MODEL, D_FF, TILE_FRAMES
    nsteps = seq // tt
    nat_spec = pl.BlockSpec((bsz, tt, d), lambda i: (0, i, 0))
    if slab_in:
        in_spec = pl.BlockSpec((N_SLABS, tt * bsz, LANES), lambda i: (0, i, 0))
        out_spec = nat_spec
        out_shape = jax.ShapeDtypeStruct((bsz, seq, d), F32)
    else:
        in_spec = nat_spec
        out_spec = pl.BlockSpec((N_SLABS, None, bsz * PLANE_PITCH, LANES), lambda i: (0, i, 0, 0))
        out_shape = jax.ShapeDtypeStruct((N_SLABS, nsteps, bsz * PLANE_PITCH, LANES), F32)
    plans = [_cast_plan(w.shape[0], nsteps) for w in cast]
    cast_specs = [pl.BlockSpec((chunk, w.shape[1]), lambda i, last=n - 1: (jnp.minimum(i, last), 0))
                  for w, (chunk, n) in zip(cast, plans)]
    outs = pl.pallas_call(
        functools.partial(_ffn_kernel, sublayer=sublayer, slab_in=slab_in, final_norm=final_norm,
                          cast_chunks=tuple(n for _, n in plans)),
        grid=(nsteps,),
        in_specs=[
            in_spec,
            _resident((N_MOD_ROWS, bsz, d)),
            _resident((1, d)),
            _resident((d, 2 * f)),
            _resident((f, d)),
            _resident((1, d)),
        ] + cast_specs,
        out_specs=[out_spec] + cast_specs,
        out_shape=[out_shape] + [jax.ShapeDtypeStruct(w.shape, BF16) for w in cast],
        compiler_params=pltpu.CompilerParams(
            dimension_semantics=("arbitrary",), vmem_limit_bytes=VMEM_LIMIT_BYTES),
        name=f"ffn{sublayer}",
    )(x, mod, g.reshape(1, d), w_in, w_out, g_final.reshape(1, d), *cast)
    return outs[0], tuple(outs[1:])


def _mixer_kernel(x_ref, mod_ref, g_ref, win_ref, pw_ref, pb_ref, ps_ref, wpu_ref,
                  btre_ref, btim_ref, are_ref, aim_ref, ctre_ref, ctim_ref, dsk_ref,
                  wglu_ref, bglu_ref, wsu_ref, wout_ref, o_ref,
                  hist_ref, state_ref, bu_ref):
    step = pl.program_id(0)
    bsz, tt, d = SUBLANES, TILE_FRAMES, D_MODEL
    rows = tt * bsz
    halo = HALO_FRAMES * bsz
    shift, scale, gate = (mod_ref[3 * MIX_SUBLAYER + k] for k in range(3))
    tile3 = lambda v: v.reshape(tt, bsz, v.shape[-1])
    flat = lambda v: v.reshape(rows, v.shape[-1])

    @pl.when(step == 0)
    def _():
        state_ref[...] = jnp.zeros_like(state_ref)
        hist_ref[0:halo, :] = jnp.zeros((halo, POOL_WIDTH), F32)

    x = jnp.concatenate(
        [jnp.concatenate([x_ref[s, pl.ds(t, bsz, stride=PLANE_PITCH), :] for t in range(tt)], axis=0)
         for s in range(N_SLABS)], axis=-1)
    gs = g_ref[...] * (1.0 + scale)
    hb = flat(tile3(_rms_scale(x)) * gs[None] + shift[None]).astype(BF16)
    u_pool = _dot(hb, win_ref[:, 0:POOL_WIDTH])
    u_ssm = _dot(hb, win_ref[:, POOL_WIDTH:POOL_WIDTH + SSM_WIDTH])
    g0 = POOL_WIDTH + SSM_WIDTH

    hist_ref[halo:halo + rows, :] = u_pool
    frame = step * tt + lax.shift_right_logical(
        lax.broadcasted_iota(jnp.int32, (rows, 1), 0), int(math.log2(bsz)))
    zs = []
    for gi, w in enumerate(POOL_WINDOWS):
        lo = gi * POOL_GROUP
        cur = u_pool[:, lo:lo + POOL_GROUP]
        tot = cur
        for j in range(1, w):
            tot = tot + hist_ref[halo - j * bsz:halo - j * bsz + rows, lo:lo + POOL_GROUP]
        cnt = jnp.minimum(frame + 1, w).astype(F32)
        z = (tot / cnt - cur).astype(BF16)
        pz = _dot(z, pw_ref[gi]) + pb_ref[:, lo:lo + POOL_GROUP]
        zs.append(pz * ps_ref[:, lo:lo + POOL_GROUP])
    hist_ref[0:halo, :] = hist_ref[rows:rows + halo, :]
    pooled = jnp.concatenate(zs, axis=-1).astype(BF16)
    merged = jax.nn.sigmoid(_dot(hb, win_ref[:, g0:g0 + d])) * _dot(pooled, wpu_ref[...])
    gl_ssm = jax.nn.sigmoid(_dot(hb, win_ref[:, g0 + d:g0 + 2 * d]))

    ub = u_ssm.astype(BF16)
    kw = btre_ref.shape[1]
    for j in range(btre_ref.shape[0]):
        k0 = (j * MXU_DIM // (SSM_STATE // SSM_GROUP)) // kw * kw
        uk = ub[:, k0:k0 + kw]
        bu_ref[:, j * MXU_DIM:(j + 1) * MXU_DIM] = _dot(uk, btre_ref[j])
        bu_ref[:, N_STATES + j * MXU_DIM:N_STATES + (j + 1) * MXU_DIM] = _dot(uk, btim_ref[j])

    for c in range(N_STATES // SCAN_LANES):
        re = slice(c * SCAN_LANES, (c + 1) * SCAN_LANES)
        im = slice(N_STATES + c * SCAN_LANES, N_STATES + (c + 1) * SCAN_LANES)
        a_re, a_im = are_ref[:, re], aim_ref[:, re]
        s_re, s_im = state_ref[:, re], state_ref[:, im]
        for t in range(tt):
            fr = slice(t * bsz, (t + 1) * bsz)
            n_re = a_re * s_re - a_im * s_im + bu_ref[fr, re]
            n_im = a_re * s_im + a_im * s_re + bu_ref[fr, im]
            bu_ref[fr, re] = n_re
            bu_ref[fr, im] = n_im
            s_re, s_im = n_re, n_im
        state_ref[:, re] = s_re
        state_ref[:, im] = s_im

    kc = ctre_ref.shape[1]
    ys = []
    for j in range(ctre_ref.shape[0]):
        s_re = bu_ref[:, j * kc:(j + 1) * kc].astype(BF16)
        s_im = bu_ref[:, N_STATES + j * kc:N_STATES + (j + 1) * kc].astype(BF16)
        ys.append(_dot(s_re, ctre_ref[j]) - _dot(s_im, ctim_ref[j]))
    y = jnp.concatenate(ys, axis=-1) + dsk_ref[...] * u_ssm
    y = 0.5 * y * (1.0 + lax.erf(y * (1.0 / math.sqrt(2.0))))
    vg = _dot(y.astype(BF16), wglu_ref[...]) + bglu_ref[...]
    glu = vg[:, 0:SSM_WIDTH] * jax.nn.sigmoid(vg[:, SSM_WIDTH:2 * SSM_WIDTH])
    merged = merged + gl_ssm * _dot(glu.astype(BF16), wsu_ref[...])
    mix = _dot(merged.astype(BF16), wout_ref[...])
    out = x + flat(gate[None] * tile3(mix))
    for s in range(N_SLABS):
        o_ref[s] = out[:, s * LANES:(s + 1) * LANES]


def _drive_tiles(bb):
    g, h, n = bb.shape
    gpt = MXU_DIM // n
    gpw = B_TILE_K // h
    tpw = gpw // gpt
    ntile = g // gpt
    bb5 = bb.reshape(ntile // tpw, tpw, gpt, h, n)
    place = (jnp.arange(gpw)[None, :, None]
             == gpt * jnp.arange(tpw)[:, None, None] + jnp.arange(gpt)[None, None, :])
    tiles = jnp.einsum("jpqhn,prq->jprhqn", bb5, place.astype(bb.dtype))
    return tiles.reshape(ntile, B_TILE_K, MXU_DIM).astype(BF16)


def _readout_tiles(cc):
    g, h, n = cc.shape
    nt = SSM_WIDTH // MXU_DIM
    gpt = g // nt
    eye = jnp.eye(gpt, dtype=cc.dtype)
    tiles = jnp.einsum("jghn,gk->jgnkh", cc.reshape(nt, gpt, h, n), eye)
    return tiles.reshape(nt, gpt * n, MXU_DIM).astype(BF16)


def _mixer(x_planes, mod, g_mix, w_in, pool_w, pool_b, pool_scale, w_pool_up,
           ab_re, ab_im, bb_re, bb_im, c_re, c_im, d_skip, w_glu, b_glu, w_ssm_up, w_out,
           *, bsz, seq):
    d, tt = D_MODEL, TILE_FRAMES
    rows = tt * bsz
    lane = lambda a: jnp.broadcast_to(a.reshape(1, N_STATES), (bsz, N_STATES))
    args = (
        x_planes, mod, g_mix.reshape(1, d), w_in, pool_w.astype(BF16),
        pool_b.reshape(1, POOL_WIDTH), pool_scale.reshape(1, POOL_WIDTH), w_pool_up,
        _drive_tiles(bb_re), _drive_tiles(bb_im), lane(ab_re), lane(ab_im),
        _readout_tiles(c_re), _readout_tiles(c_im), d_skip.reshape(1, SSM_WIDTH),
        w_glu, b_glu.reshape(1, 2 * SSM_WIDTH), w_ssm_up, w_out,
    )
    plane_spec = pl.BlockSpec((N_SLABS, None, bsz * PLANE_PITCH, LANES), lambda i: (0, i, 0, 0))
    slab_spec = pl.BlockSpec((N_SLABS, rows, LANES), lambda i: (0, i, 0))
    return pl.pallas_call(
        _mixer_kernel,
        grid=(seq // tt,),
        in_specs=[plane_spec] + [_resident(a.shape) for a in args[1:]],
        out_specs=slab_spec,
        out_shape=jax.ShapeDtypeStruct((N_SLABS, seq * bsz, LANES), F32),
        scratch_shapes=[
            pltpu.VMEM((rows + HALO_FRAMES * bsz, POOL_WIDTH), F32),
            pltpu.VMEM((bsz, 2 * N_STATES), F32),
            pltpu.VMEM((rows, 2 * N_STATES), F32),
        ],
        compiler_params=pltpu.CompilerParams(
            dimension_semantics=("arbitrary",), vmem_limit_bytes=VMEM_LIMIT_BYTES),
        name="mixer",
    )(*args)


def kernel(x, c, w_ada, b_ada, g_ffn1, w_ffn1_in, w_ffn1_out, g_mix, w_in, pool_w, pool_b,
           pool_scale, w_pool_up, ssm_lam_re_log, ssm_lam_im, ssm_log_dt, ssm_b_re, ssm_b_im,
           ssm_c_re, ssm_c_im, ssm_d, w_glu, b_glu, w_ssm_up, w_out, g_ffn2, w_ffn2_in,
           w_ffn2_out, g_final):
    bsz, seq, d = x.shape
    assert (bsz, d) == (SUBLANES, D_MODEL) and seq % TILE_FRAMES == 0
    assert w_ada.shape[0] == 1 and w_ada.shape[2] == N_MOD_ROWS * d, "single-layer block"
    l = 0
    mod = _adaln(c, w_ada[l], b_ada[l])
    ab_re, ab_im, bb_re, bb_im = _ssm_prep(ssm_lam_re_log[l], ssm_lam_im[l], ssm_log_dt[l],
                                           ssm_b_re[l], ssm_b_im[l])
    later = (w_in[l], w_pool_up[l], w_glu[l], w_ssm_up[l], w_out[l], w_ffn2_in[l], w_ffn2_out[l])
    x1, later_bf16 = _ffn(x, mod, g_ffn1[l], w_ffn1_in[l].astype(BF16), w_ffn1_out[l].astype(BF16),
                          g_final, sublayer=0, slab_in=False, final_norm=False, bsz=bsz, seq=seq,
                          cast=later)
    w_in_b, w_pool_up_b, w_glu_b, w_ssm_up_b, w_out_b, w_ffn2_in_b, w_ffn2_out_b = later_bf16
    x2 = _mixer(x1, mod, g_mix[l], w_in_b, pool_w[l], pool_b[l], pool_scale[l], w_pool_up_b,
                ab_re, ab_im, bb_re, bb_im, ssm_c_re[l], ssm_c_im[l], ssm_d[l],
                w_glu_b, b_glu[l], w_ssm_up_b, w_out_b, bsz=bsz, seq=seq)
    out, _ = _ffn(x2, mod, g_ffn2[l], w_ffn2_in_b, w_ffn2_out_b, g_final,
                  sublayer=2, slab_in=True, final_norm=True, bsz=bsz, seq=seq)
    return out
```

```python
import functools
import math

import jax
import jax.numpy as jnp
from jax import lax
from jax.experimental import pallas as pl
from jax.experimental.pallas import tpu as pltpu

D_MODEL = 1024
D_FF = 2816
N_MOD_ROWS = 9
MIX_SUBLAYER = 1
POOL_WIDTH = 512
POOL_WINDOWS = (2, 4, 8, 16)
POOL_GROUP = 128
SSM_WIDTH = 512
SSM_GROUP = 16
N_SSM_GROUPS = 32
SSM_STATE = 64
N_STATES = N_SSM_GROUPS * SSM_STATE
EPS = 1e-6

F32 = jnp.float32
BF16 = jnp.bfloat16

SUBLANES = 8
LANES = 128
MXU_DIM = 256
BF16_TILE_ROWS = 16
VMEM_LIMIT_BYTES = 60000 * 1024

N_SLABS = D_MODEL // LANES
TILE_FRAMES = 64
PLANE_PITCH = TILE_FRAMES + SUBLANES
FFN_FC = MXU_DIM
HALO_FRAMES = 16
SCAN_LANES = 512
B_TILE_K = LANES


def _resident(shape):
    nd = len(shape)
    return pl.BlockSpec(shape, lambda *_: (0,) * nd, pipeline_mode=pl.Buffered(1))


def _rms_scale(x):
    return x * lax.rsqrt(jnp.mean(x * x, axis=-1, keepdims=True) + EPS)


def _dot(a, b):
    return jnp.dot(a, b, preferred_element_type=F32)


def _adaln_kernel(c_ref, w_ref, b_ref, o_ref):
    c = c_ref[...]
    s = c * jax.nn.sigmoid(c)
    o_ref[...] = _dot(s.astype(BF16), w_ref[...].astype(BF16)) + b_ref[...]


def _adaln(c, w, b):
    bsz, d = c.shape
    n = w.shape[1]
    return pl.pallas_call(
        _adaln_kernel,
        grid=(n // d,),
        in_specs=[
            pl.BlockSpec((bsz, d), lambda j: (0, 0)),
            pl.BlockSpec((d, d), lambda j: (0, j)),
            pl.BlockSpec((1, d), lambda j: (0, j)),
        ],
        out_specs=pl.BlockSpec((None, bsz, d), lambda j: (j, 0, 0)),
        out_shape=jax.ShapeDtypeStruct((n // d, bsz, d), F32),
        compiler_params=pltpu.CompilerParams(dimension_semantics=("arbitrary",)),
        name="adaln",
    )(c, w, b.reshape(1, n))


def _ssm_prep_kernel(lre_ref, lim_ref, ldt_ref, bre_ref, bim_ref,
                     are_ref, aim_ref, bbre_ref, bbim_ref):
    lr = -jnp.exp(lre_ref[...])
    li = lim_ref[...]
    dt = jnp.exp(ldt_ref[...])
    mag = jnp.exp(lr * dt)
    ang = li * dt
    ab_re = mag * jnp.cos(ang)
    ab_im = mag * jnp.sin(ang)
    num_re = ab_re - 1.0
    num_im = ab_im
    den = lr * lr + li * li
    f_re = (num_re * lr + num_im * li) / den
    f_im = (num_im * lr - num_re * li) / den
    br = bre_ref[...]
    bi = bim_ref[...]
    are_ref[...] = ab_re
    aim_ref[...] = ab_im
    bbre_ref[...] = f_re * br - f_im * bi
    bbim_ref[...] = f_re * bi + f_im * br


def _ssm_prep(lam_re_log, lam_im, log_dt, b_re, b_im):
    g, n, h = b_re.shape
    per_state = lambda a: a.reshape(g, 1, n)
    ldt = jnp.broadcast_to(log_dt[:, None, None], (g, 1, n))
    full = lambda shape: pl.BlockSpec(shape, lambda: (0,) * len(shape))
    return pl.pallas_call(
        _ssm_prep_kernel,
        in_specs=[full((g, 1, n))] * 3 + [full((g, h, n))] * 2,
        out_specs=[full((g, 1, n))] * 2 + [full((g, h, n))] * 2,
        out_shape=[jax.ShapeDtypeStruct((g, 1, n), F32)] * 2
        + [jax.ShapeDtypeStruct((g, h, n), F32)] * 2,
        name="ssm_prep",
    )(per_state(lam_re_log), per_state(lam_im), ldt,
      b_re.transpose(0, 2, 1), b_im.transpose(0, 2, 1))


def _cast_plan(rows, nsteps):
    chunk = BF16_TILE_ROWS
    while rows % chunk or rows // chunk > nsteps:
        chunk += BF16_TILE_ROWS
    return chunk, rows // chunk


def _zero_tile(v):
    bits = pltpu.bitcast(v, jnp.uint32)
    acc = None
    for r in range(0, bits.shape[0], SUBLANES):
        for c in range(0, bits.shape[1], LANES):
            t = bits[r:r + SUBLANES, c:c + LANES]
            acc = t if acc is None else acc | t
    half = jnp.uint32(16)
    return lax.shift_right_logical(lax.shift_right_logical(acc, half), half).astype(F32)


def _ffn_kernel(x_ref, xn_ref, mod_ref, g_ref, win_ref, wout_ref, gfin_ref, *rest,
                sublayer, slab_in, final_norm, cast_chunks, nmain):
    ncast = len(cast_chunks)
    cast_in, o_ref, cast_out = rest[:ncast], rest[ncast], rest[ncast + 1:2 * ncast + 1]
    ha_ref, hb_ref, pre_ref, rs_ref = rest[2 * ncast + 1:]
    bsz, tt, fc = SUBLANES, TILE_FRAMES, FFN_FC
    rows = bsz * tt
    nc = D_FF // fc
    row = lambda b, k: mod_ref[3 * sublayer + k, b:b + 1, :]
    seq_rows = lambda b: slice(b * tt, (b + 1) * tt)
    step = pl.program_id(0)

    for src, dst, nchunks in zip(cast_in, cast_out, cast_chunks):
        @pl.when(step < nchunks)
        def _(src=src, dst=dst):
            dst[...] = src[...].astype(BF16)

    def load_x(ref, b, lo, hi):
        if slab_in:
            return jnp.concatenate(
                [ref[s, pl.ds(b, tt, stride=bsz), :] for s in range(lo // LANES, hi // LANES)],
                axis=-1)
        return ref[b, :, lo:hi]

    def normalise(ref, h_ref, b):
        gs = g_ref[...] * (1.0 + row(b, 1))
        h = (_rms_scale(load_x(ref, b, 0, D_MODEL)) * gs + row(b, 0)).astype(BF16)
        h_ref[seq_rows(b), :] = h
        return h

    def finish(b):
        y = pre_ref[seq_rows(b), :] * rs_ref[seq_rows(b), :] * gfin_ref[...]
        o_ref[b] = y
        return y

    @pl.when(step == 0)
    def _():
        for b in range(bsz):
            normalise(x_ref, ha_ref, b)
        if final_norm:
            pre_ref[...] = jnp.zeros_like(pre_ref)
            rs_ref[...] = jnp.zeros_like(rs_ref)

    if not slab_in:
        for s in range(N_SLABS):
            for b in range(bsz):
                o_ref[s, b * PLANE_PITCH + tt:(b + 1) * PLANE_PITCH, :] = \
                    jnp.zeros((PLANE_PITCH - tt, LANES), F32)

    def main(h_cur, h_nxt):
        hb = h_cur[...]
        gl = []
        for c in range(nc):
            a = _dot(hb, win_ref[:, c * fc:(c + 1) * fc])
            bb = _dot(hb, win_ref[:, D_FF + c * fc:D_FF + (c + 1) * fc])
            if c < bsz:
                z = _zero_tile(normalise(xn_ref, h_nxt, c))
                if final_norm:
                    z = z + _zero_tile(finish(c))
                z = jnp.concatenate([z] * (fc // LANES), axis=1)
                a = (a.reshape(rows // SUBLANES, SUBLANES, fc) + z[None]).reshape(rows, fc)
            gl.append((a * jax.nn.sigmoid(a) * bb).astype(BF16))
        glu = jnp.concatenate(gl, axis=-1)
        sumsq = [None] * bsz
        for n in range(D_MODEL // fc):
            lo = n * fc
            r = _dot(glu, wout_ref[:, lo:lo + fc])
            for b in range(bsz):
                out = load_x(x_ref, b, lo, lo + fc) + (0.5 * row(b, 2)[:, lo:lo + fc]) * r[seq_rows(b)]
                if final_norm:
                    pre_ref[seq_rows(b), lo:lo + fc] = out
                    ss = jnp.sum(out * out, axis=-1, keepdims=True)
                    sumsq[b] = ss if sumsq[b] is None else sumsq[b] + ss
                elif slab_in:
                    o_ref[b, :, lo:lo + fc] = out
                else:
                    for s in range(fc // LANES):
                        o_ref[lo // LANES + s, b * PLANE_PITCH:b * PLANE_PITCH + tt, :] = \
                            out[:, s * LANES:(s + 1) * LANES]
        if final_norm:
            for b in range(bsz):
                rs_ref[seq_rows(b), :] = lax.rsqrt(sumsq[b] * (1.0 / D_MODEL) + EPS)

    even = lax.rem(step, 2) == 0
    pl.when(jnp.logical_and(step < nmain, even))(lambda: main(ha_ref, hb_ref))
    pl.when(jnp.logical_and(step < nmain, jnp.logical_not(even)))(lambda: main(hb_ref, ha_ref))

    if final_norm:
        @pl.when(step == nmain)
        def _():
            for b in range(bsz):
                finish(b)


def _ffn(x, mod, g, w_in, w_out, g_final, *, sublayer, slab_in, final_norm, bsz, seq, cast=()):
    d, f, tt = D_MODEL, D_FF, TILE_FRAMES
    nmain = seq // tt
    lag = 1 if final_norm else 0
    nsteps = nmain + lag
    cur = lambda i: jnp.minimum(i, nmain - 1)
    nxt = lambda i: jnp.minimum(i + 1, nmain - 1)
    done = lambda i: jnp.maximum(i - lag, 0)
    if slab_in:
        x_block = (N_SLABS, tt * bsz, LANES)
        x_specs = [pl.BlockSpec(x_block, lambda i: (0, cur(i), 0)),
                   pl.BlockSpec(x_block, lambda i: (0, nxt(i), 0))]
        out_spec = pl.BlockSpec((bsz, tt, d), lambda i: (0, done(i), 0))
        out_shape = jax.ShapeDtypeStruct((bsz, seq, d), F32)
    else:
        x_block = (bsz, tt, d)
        x_specs = [pl.BlockSpec(x_block, lambda i: (0, cur(i), 0)),
                   pl.BlockSpec(x_block, lambda i: (0, nxt(i), 0))]
        out_spec = pl.BlockSpec((N_SLABS, None, bsz * PLANE_PITCH, LANES),
                                lambda i: (0, done(i), 0, 0))
        out_shape = jax.ShapeDtypeStruct((N_SLABS, nmain, bsz * PLANE_PITCH, LANES), F32)
    plans = [_cast_plan(w.shape[0], nsteps) for w in cast]
    cast_specs = [pl.BlockSpec((chunk, w.shape[1]), lambda i, last=n - 1: (jnp.minimum(i, last), 0))
                  for w, (chunk, n) in zip(cast, plans)]
    outs = pl.pallas_call(
        functools.partial(_ffn_kernel, sublayer=sublayer, slab_in=slab_in, final_norm=final_norm,
                          cast_chunks=tuple(n for _, n in plans), nmain=nmain),
        grid=(nsteps,),
        in_specs=x_specs + [
            _resident((N_MOD_ROWS, bsz, d)),
            _resident((1, d)),
            _resident((d, 2 * f)),
            _resident((f, d)),
            _resident((1, d)),
        ] + cast_specs,
        out_specs=[out_spec] + cast_specs,
        out_shape=[out_shape] + [jax.ShapeDtypeStruct(w.shape, BF16) for w in cast],
        scratch_shapes=[
            pltpu.VMEM((tt * bsz, d), BF16),
            pltpu.VMEM((tt * bsz, d), BF16),
            pltpu.VMEM((tt * bsz, d), F32),
            pltpu.VMEM((tt * bsz, 1), F32),
        ],
        compiler_params=pltpu.CompilerParams(
            dimension_semantics=("arbitrary",), vmem_limit_bytes=VMEM_LIMIT_BYTES),
        name=f"ffn{sublayer}",
    )(x, x, mod, g.reshape(1, d), w_in, w_out, g_final.reshape(1, d), *cast)
    return outs[0], tuple(outs[1:])


def _mixer_kernel(x_ref, mod_ref, g_ref, win_ref, pw_ref, pb_ref, ps_ref, wpu_ref,
                  btre_ref, btim_ref, are_ref, aim_ref, ctre_ref, ctim_ref, dsk_ref,
                  wglu_ref, bglu_ref, wsu_ref, wout_ref, o_ref,
                  hist_ref, state_ref, bu_ref):
    step = pl.program_id(0)
    bsz, tt, d = SUBLANES, TILE_FRAMES, D_MODEL
    rows = tt * bsz
    halo = HALO_FRAMES * bsz
    shift, scale, gate = (mod_ref[3 * MIX_SUBLAYER + k] for k in range(3))
    tile3 = lambda v: v.reshape(tt, bsz, v.shape[-1])
    flat = lambda v: v.reshape(rows, v.shape[-1])

    @pl.when(step == 0)
    def _():
        state_ref[...] = jnp.zeros_like(state_ref)
        hist_ref[0:halo, :] = jnp.zeros((halo, POOL_WIDTH), F32)

    x = jnp.concatenate(
        [jnp.concatenate([x_ref[s, pl.ds(t, bsz, stride=PLANE_PITCH), :] for t in range(tt)], axis=0)
         for s in range(N_SLABS)], axis=-1)
    gs = g_ref[...] * (1.0 + scale)
    hb = flat(tile3(_rms_scale(x)) * gs[None] + shift[None]).astype(BF16)
    u_pool = _dot(hb, win_ref[:, 0:POOL_WIDTH])
    u_ssm = _dot(hb, win_ref[:, POOL_WIDTH:POOL_WIDTH + SSM_WIDTH])
    g0 = POOL_WIDTH + SSM_WIDTH

    hist_ref[halo:halo + rows, :] = u_pool
    frame = step * tt + lax.shift_right_logical(
        lax.broadcasted_iota(jnp.int32, (rows, 1), 0), int(math.log2(bsz)))
    zs = []
    for gi, w in enumerate(POOL_WINDOWS):
        lo = gi * POOL_GROUP
        cur = u_pool[:, lo:lo + POOL_GROUP]
        tot = cur
        for j in range(1, w):
            tot = tot + hist_ref[halo - j * bsz:halo - j * bsz + rows, lo:lo + POOL_GROUP]
        cnt = jnp.minimum(frame + 1, w).astype(F32)
        z = (tot / cnt - cur).astype(BF16)
        pz = _dot(z, pw_ref[gi]) + pb_ref[:, lo:lo + POOL_GROUP]
        zs.append(pz * ps_ref[:, lo:lo + POOL_GROUP])
    hist_ref[0:halo, :] = hist_ref[rows:rows + halo, :]
    pooled = jnp.concatenate(zs, axis=-1).astype(BF16)
    merged = jax.nn.sigmoid(_dot(hb, win_ref[:, g0:g0 + d])) * _dot(pooled, wpu_ref[...])
    gl_ssm = jax.nn.sigmoid(_dot(hb, win_ref[:, g0 + d:g0 + 2 * d]))

    ub = u_ssm.astype(BF16)
    kw = btre_ref.shape[1]
    for j in range(btre_ref.shape[0]):
        k0 = (j * MXU_DIM // (SSM_STATE // SSM_GROUP)) // kw * kw
        uk = ub[:, k0:k0 + kw]
        bu_ref[:, j * MXU_DIM:(j + 1) * MXU_DIM] = _dot(uk, btre_ref[j])
        bu_ref[:, N_STATES + j * MXU_DIM:N_STATES + (j + 1) * MXU_DIM] = _dot(uk, btim_ref[j])

    for c in range(N_STATES // SCAN_LANES):
        re = slice(c * SCAN_LANES, (c + 1) * SCAN_LANES)
        im = slice(N_STATES + c * SCAN_LANES, N_STATES + (c + 1) * SCAN_LANES)
        a_re, a_im = are_ref[:, re], aim_ref[:, re]
        s_re, s_im = state_ref[:, re], state_ref[:, im]
        for t in range(tt):
            fr = slice(t * bsz, (t + 1) * bsz)
            n_re = a_re * s_re - a_im * s_im + bu_ref[fr, re]
            n_im = a_re * s_im + a_im * s_re + bu_ref[fr, im]
            bu_ref[fr, re] = n_re
            bu_ref[fr, im] = n_im
            s_re, s_im = n_re, n_im
        state_ref[:, re] = s_re
        state_ref[:, im] = s_im

    kc = ctre_ref.shape[1]
    ys = []
    for j in range(ctre_ref.shape[0]):
        s_re = bu_ref[:, j * kc:(j + 1) * kc].astype(BF16)
        s_im = bu_ref[:, N_STATES + j * kc:N_STATES + (j + 1) * kc].astype(BF16)
        ys.append(_dot(s_re, ctre_ref[j]) - _dot(s_im, ctim_ref[j]))
    y = jnp.concatenate(ys, axis=-1) + dsk_ref[...] * u_ssm
    y = 0.5 * y * (1.0 + lax.erf(y * (1.0 / math.sqrt(2.0))))
    vg = _dot(y.astype(BF16), wglu_ref[...]) + bglu_ref[...]
    glu = vg[:, 0:SSM_WIDTH] * jax.nn.sigmoid(vg[:, SSM_WIDTH:2 * SSM_WIDTH])
    merged = merged + gl_ssm * _dot(glu.astype(BF16), wsu_ref[...])
    mix = _dot(merged.astype(BF16), wout_ref[...])
    out = x + flat(gate[None] * tile3(mix))
    for s in range(N_SLABS):
        o_ref[s] = out[:, s * LANES:(s + 1) * LANES]


def _drive_tiles(bb):
    g, h, n = bb.shape
    gpt = MXU_DIM // n
    gpw = B_TILE_K // h
    tpw = gpw // gpt
    ntile = g // gpt
    bb5 = bb.reshape(ntile // tpw, tpw, gpt, h, n)
    place = (jnp.arange(gpw)[None, :, None]
             == gpt * jnp.arange(tpw)[:, None, None] + jnp.arange(gpt)[None, None, :])
    tiles = jnp.einsum("jpqhn,prq->jprhqn", bb5, place.astype(bb.dtype))
    return tiles.reshape(ntile, B_TILE_K, MXU_DIM).astype(BF16)


def _readout_tiles(cc):
    g, h, n = cc.shape
    nt = SSM_WIDTH // MXU_DIM
    gpt = g // nt
    eye = jnp.eye(gpt, dtype=cc.dtype)
    tiles = jnp.einsum("jghn,gk->jgnkh", cc.reshape(nt, gpt, h, n), eye)
    return tiles.reshape(nt, gpt * n, MXU_DIM).astype(BF16)


def _mixer(x_planes, mod, g_mix, w_in, pool_w, pool_b, pool_scale, w_pool_up,
           ab_re, ab_im, bb_re, bb_im, c_re, c_im, d_skip, w_glu, b_glu, w_ssm_up, w_out,
           *, bsz, seq):
    d, tt = D_MODEL, TILE_FRAMES
    rows = tt * bsz
    lane = lambda a: jnp.broadcast_to(a.reshape(1, N_STATES), (bsz, N_STATES))
    args = (
        x_planes, mod, g_mix.reshape(1, d), w_in, pool_w.astype(BF16),
        pool_b.reshape(1, POOL_WIDTH), pool_scale.reshape(1, POOL_WIDTH), w_pool_up,
        _drive_tiles(bb_re), _drive_tiles(bb_im), lane(ab_re), lane(ab_im),
        _readout_tiles(c_re), _readout_tiles(c_im), d_skip.reshape(1, SSM_WIDTH),
        w_glu, b_glu.reshape(1, 2 * SSM_WIDTH), w_ssm_up, w_out,
    )
    plane_spec = pl.BlockSpec((N_SLABS, None, bsz * PLANE_PITCH, LANES), lambda i: (0, i, 0, 0))
    slab_spec = pl.BlockSpec((N_SLABS, rows, LANES), lambda i: (0, i, 0))
    return pl.pallas_call(
        _mixer_kernel,
        grid=(seq // tt,),
        in_specs=[plane_spec] + [_resident(a.shape) for a in args[1:]],
        out_specs=slab_spec,
        out_shape=jax.ShapeDtypeStruct((N_SLABS, seq * bsz, LANES), F32),
        scratch_shapes=[
            pltpu.VMEM((rows + HALO_FRAMES * bsz, POOL_WIDTH), F32),
            pltpu.VMEM((bsz, 2 * N_STATES), F32),
            pltpu.VMEM((rows, 2 * N_STATES), F32),
        ],
        compiler_params=pltpu.CompilerParams(
            dimension_semantics=("arbitrary",), vmem_limit_bytes=VMEM_LIMIT_BYTES),
        name="mixer",
    )(*args)


def kernel(x, c, w_ada, b_ada, g_ffn1, w_ffn1_in, w_ffn1_out, g_mix, w_in, pool_w, pool_b,
           pool_scale, w_pool_up, ssm_lam_re_log, ssm_lam_im, ssm_log_dt, ssm_b_re, ssm_b_im,
           ssm_c_re, ssm_c_im, ssm_d, w_glu, b_glu, w_ssm_up, w_out, g_ffn2, w_ffn2_in,
           w_ffn2_out, g_final):
    bsz, seq, d = x.shape
    assert (bsz, d) == (SUBLANES, D_MODEL) and seq % TILE_FRAMES == 0
    assert w_ada.shape[0] == 1 and w_ada.shape[2] == N_MOD_ROWS * d, "single-layer block"
    l = 0
    mod = _adaln(c, w_ada[l], b_ada[l])
    ab_re, ab_im, bb_re, bb_im = _ssm_prep(ssm_lam_re_log[l], ssm_lam_im[l], ssm_log_dt[l],
                                           ssm_b_re[l], ssm_b_im[l])
    later = (w_in[l], w_pool_up[l], w_glu[l], w_ssm_up[l], w_out[l], w_ffn2_in[l], w_ffn2_out[l])
    x1, later_bf16 = _ffn(x, mod, g_ffn1[l], w_ffn1_in[l].astype(BF16), w_ffn1_out[l].astype(BF16),
                          g_final, sublayer=0, slab_in=False, final_norm=False, bsz=bsz, seq=seq,
                          cast=later)
    w_in_b, w_pool_up_b, w_glu_b, w_ssm_up_b, w_out_b, w_ffn2_in_b, w_ffn2_out_b = later_bf16
    x2 = _mixer(x1, mod, g_mix[l], w_in_b, pool_w[l], pool_b[l], pool_scale[l], w_pool_up_b,
                ab_re, ab_im, bb_re, bb_im, ssm_c_re[l], ssm_c_im[l], ssm_d[l],
                w_glu_b, b_glu[l], w_ssm_up_b, w_out_b, bsz=bsz, seq=seq)
    out, _ = _ffn(x2, mod, g_ffn2[l], w_ffn2_in_b, w_ffn2_out_b, g_final,
                  sublayer=2, slab_in=True, final_norm=True, bsz=bsz, seq=seq)
    return out
```

```python
import functools
import math

import jax
import jax.numpy as jnp
from jax import lax
from jax.experimental import pallas as pl
from jax.experimental.pallas import tpu as pltpu

D_MODEL = 1024
D_FF = 2816
N_MOD_ROWS = 9
MIX_SUBLAYER = 1
POOL_WIDTH = 512
POOL_WINDOWS = (2, 4, 8, 16)
POOL_GROUP = 128
SSM_WIDTH = 512
SSM_GROUP = 16
N_SSM_GROUPS = 32
SSM_STATE = 64
N_STATES = N_SSM_GROUPS * SSM_STATE
EPS = 1e-6

F32 = jnp.float32
BF16 = jnp.bfloat16

SUBLANES = 8
LANES = 128
MXU_DIM = 256
BF16_TILE_ROWS = 16
VMEM_LIMIT_BYTES = 60000 * 1024

N_SLABS = D_MODEL // LANES
TILE_FRAMES = 64
PLANE_PITCH = TILE_FRAMES + SUBLANES
FFN_FC = MXU_DIM
HALO_FRAMES = 16
SCAN_LANES = 512
SCAN_FRAMES = 2
B_TILE_K = LANES


def _resident(shape):
    nd = len(shape)
    return pl.BlockSpec(shape, lambda *_: (0,) * nd, pipeline_mode=pl.Buffered(1))


def _rms_scale(x):
    return x * lax.rsqrt(jnp.mean(x * x, axis=-1, keepdims=True) + EPS)


def _dot(a, b):
    return jnp.dot(a, b, preferred_element_type=F32)


def _adaln_kernel(c_ref, w_ref, b_ref, o_ref):
    c = c_ref[...]
    s = c * jax.nn.sigmoid(c)
    o_ref[...] = _dot(s.astype(BF16), w_ref[...].astype(BF16)) + b_ref[...]


def _adaln(c, w, b):
    bsz, d = c.shape
    n = w.shape[1]
    return pl.pallas_call(
        _adaln_kernel,
        grid=(n // d,),
        in_specs=[
            pl.BlockSpec((bsz, d), lambda j: (0, 0)),
            pl.BlockSpec((d, d), lambda j: (0, j)),
            pl.BlockSpec((1, d), lambda j: (0, j)),
        ],
        out_specs=pl.BlockSpec((None, bsz, d), lambda j: (j, 0, 0)),
        out_shape=jax.ShapeDtypeStruct((n // d, bsz, d), F32),
        compiler_params=pltpu.CompilerParams(dimension_semantics=("arbitrary",)),
        name="adaln",
    )(c, w, b.reshape(1, n))


def _ssm_prep_kernel(lre_ref, lim_ref, ldt_ref, bre_ref, bim_ref, cre_ref, cim_ref,
                     a2re_ref, a2im_ref, bbre_ref, bbim_ref, abre_ref, abim_ref,
                     care_ref, caim_ref, cbt_ref):
    lr = -jnp.exp(lre_ref[...])
    li = lim_ref[...]
    dt = jnp.exp(ldt_ref[...])
    mag = jnp.exp(lr * dt)
    ang = li * dt
    ab_re = mag * jnp.cos(ang)
    ab_im = mag * jnp.sin(ang)
    num_re = ab_re - 1.0
    num_im = ab_im
    den = lr * lr + li * li
    f_re = (num_re * lr + num_im * li) / den
    f_im = (num_im * lr - num_re * li) / den
    br = bre_ref[...]
    bi = bim_ref[...]
    bb_re = f_re * br - f_im * bi
    bb_im = f_re * bi + f_im * br
    c_re = cre_ref[...]
    c_im = cim_ref[...]
    a2re_ref[...] = ab_re * ab_re - ab_im * ab_im
    a2im_ref[...] = 2.0 * ab_re * ab_im
    bbre_ref[...] = bb_re
    bbim_ref[...] = bb_im
    abre_ref[...] = ab_re * bb_re - ab_im * bb_im
    abim_ref[...] = ab_re * bb_im + ab_im * bb_re
    care_ref[...] = c_re * ab_re - c_im * ab_im
    caim_ref[...] = c_re * ab_im + c_im * ab_re
    lane = lax.broadcasted_iota(jnp.int32, cbt_ref.shape, 2)
    cbt = jnp.zeros(cbt_ref.shape, F32)
    for h in range(cbt_ref.shape[2]):
        col = jnp.sum(c_re[:, h:h + 1, :] * bb_re - c_im[:, h:h + 1, :] * bb_im,
                      axis=-1, keepdims=True)
        cbt = jnp.where(lane == h, col, cbt)
    cbt_ref[...] = cbt


def _ssm_prep(lam_re_log, lam_im, log_dt, b_re, b_im, c_re, c_im):
    g, n, h = b_re.shape
    per_state = lambda a: a.reshape(g, 1, n)
    ldt = jnp.broadcast_to(log_dt[:, None, None], (g, 1, n))
    full = lambda shape: pl.BlockSpec(shape, lambda: (0,) * len(shape))
    return pl.pallas_call(
        _ssm_prep_kernel,
        in_specs=[full((g, 1, n))] * 3 + [full((g, h, n))] * 4,
        out_specs=[full((g, 1, n))] * 2 + [full((g, h, n))] * 6 + [full((g, h, h))],
        out_shape=[jax.ShapeDtypeStruct((g, 1, n), F32)] * 2
        + [jax.ShapeDtypeStruct((g, h, n), F32)] * 6 + [jax.ShapeDtypeStruct((g, h, h), F32)],
        name="ssm_prep",
    )(per_state(lam_re_log), per_state(lam_im), ldt,
      b_re.transpose(0, 2, 1), b_im.transpose(0, 2, 1), c_re, c_im)


def _cast_plan(rows, nsteps):
    chunk = BF16_TILE_ROWS
    while rows % chunk or rows // chunk > nsteps:
        chunk += BF16_TILE_ROWS
    return chunk, rows // chunk


def _zero_tile(v):
    bits = pltpu.bitcast(v, jnp.uint32)
    acc = None
    for r in range(0, bits.shape[0], SUBLANES):
        for c in range(0, bits.shape[1], LANES):
            t = bits[r:r + SUBLANES, c:c + LANES]
            acc = t if acc is None else acc | t
    half = jnp.uint32(16)
    return lax.shift_right_logical(lax.shift_right_logical(acc, half), half).astype(F32)


def _ffn_kernel(x_ref, xn_ref, mod_ref, g_ref, win_ref, wout_ref, gfin_ref, *rest,
                sublayer, slab_in, final_norm, cast_chunks, nmain):
    ncast = len(cast_chunks)
    cast_in, o_ref, cast_out = rest[:ncast], rest[ncast], rest[ncast + 1:2 * ncast + 1]
    ha_ref, hb_ref, pre_ref, rs_ref = rest[2 * ncast + 1:]
    bsz, tt, fc = SUBLANES, TILE_FRAMES, FFN_FC
    rows = bsz * tt
    nc = D_FF // fc
    row = lambda b, k: mod_ref[3 * sublayer + k, b:b + 1, :]
    seq_rows = lambda b: slice(b * tt, (b + 1) * tt)
    step = pl.program_id(0)

    for src, dst, nchunks in zip(cast_in, cast_out, cast_chunks):
        @pl.when(step < nchunks)
        def _(src=src, dst=dst):
            dst[...] = src[...].astype(BF16)

    def load_x(ref, b, lo, hi):
        if slab_in:
            return jnp.concatenate(
                [ref[s, pl.ds(b, tt, stride=bsz), :] for s in range(lo // LANES, hi // LANES)],
                axis=-1)
        return ref[b, :, lo:hi]

    def normalise(ref, h_ref, b):
        gs = g_ref[...] * (1.0 + row(b, 1))
        h = (_rms_scale(load_x(ref, b, 0, D_MODEL)) * gs + row(b, 0)).astype(BF16)
        h_ref[seq_rows(b), :] = h
        return h

    def finish(b):
        y = pre_ref[seq_rows(b), :] * rs_ref[seq_rows(b), :] * gfin_ref[...]
        o_ref[b] = y
        return y

    @pl.when(step == 0)
    def _():
        for b in range(bsz):
            normalise(x_ref, ha_ref, b)
        if final_norm:
            pre_ref[...] = jnp.zeros_like(pre_ref)
            rs_ref[...] = jnp.zeros_like(rs_ref)

    if not slab_in:
        for s in range(N_SLABS):
            for b in range(bsz):
                o_ref[s, b * PLANE_PITCH + tt:(b + 1) * PLANE_PITCH, :] = \
                    jnp.zeros((PLANE_PITCH - tt, LANES), F32)

    def main(h_cur, h_nxt):
        hb = h_cur[...]
        gl = []
        for c in range(nc):
            a = _dot(hb, win_ref[:, c * fc:(c + 1) * fc])
            bb = _dot(hb, win_ref[:, D_FF + c * fc:D_FF + (c + 1) * fc])
            if c < bsz:
                z = _zero_tile(normalise(xn_ref, h_nxt, c))
                if final_norm:
                    z = z + _zero_tile(finish(c))
                z = jnp.concatenate([z] * (fc // LANES), axis=1)
                a = (a.reshape(rows // SUBLANES, SUBLANES, fc) + z[None]).reshape(rows, fc)
            gl.append((a * jax.nn.sigmoid(a) * bb).astype(BF16))
        glu = jnp.concatenate(gl, axis=-1)
        sumsq = [None] * bsz
        for n in range(D_MODEL // fc):
            lo = n * fc
            r = _dot(glu, wout_ref[:, lo:lo + fc])
            for b in range(bsz):
                out = load_x(x_ref, b, lo, lo + fc) + (0.5 * row(b, 2)[:, lo:lo + fc]) * r[seq_rows(b)]
                if final_norm:
                    pre_ref[seq_rows(b), lo:lo + fc] = out
                    ss = jnp.sum(out * out, axis=-1, keepdims=True)
                    sumsq[b] = ss if sumsq[b] is None else sumsq[b] + ss
                elif slab_in:
                    o_ref[b, :, lo:lo + fc] = out
                else:
                    for s in range(fc // LANES):
                        o_ref[lo // LANES + s, b * PLANE_PITCH:b * PLANE_PITCH + tt, :] = \
                            out[:, s * LANES:(s + 1) * LANES]
        if final_norm:
            for b in range(bsz):
                rs_ref[seq_rows(b), :] = lax.rsqrt(sumsq[b] * (1.0 / D_MODEL) + EPS)

    even = lax.rem(step, 2) == 0
    pl.when(jnp.logical_and(step < nmain, even))(lambda: main(ha_ref, hb_ref))
    pl.when(jnp.logical_and(step < nmain, jnp.logical_not(even)))(lambda: main(hb_ref, ha_ref))

    if final_norm:
        @pl.when(step == nmain)
        def _():
            for b in range(bsz):
                finish(b)


def _ffn(x, mod, g, w_in, w_out, g_final, *, sublayer, slab_in, final_norm, bsz, seq, cast=()):
    d, f, tt = D_MODEL, D_FF, TILE_FRAMES
    nmain = seq // tt
    lag = 1 if final_norm else 0
    nsteps = nmain + lag
    cur = lambda i: jnp.minimum(i, nmain - 1)
    nxt = lambda i: jnp.minimum(i + 1, nmain - 1)
    done = lambda i: jnp.maximum(i - lag, 0)
    if slab_in:
        x_block = (N_SLABS, tt * bsz, LANES)
        x_specs = [pl.BlockSpec(x_block, lambda i: (0, cur(i), 0)),
                   pl.BlockSpec(x_block, lambda i: (0, nxt(i), 0))]
        out_spec = pl.BlockSpec((bsz, tt, d), lambda i: (0, done(i), 0))
        out_shape = jax.ShapeDtypeStruct((bsz, seq, d), F32)
    else:
        x_block = (bsz, tt, d)
        x_specs = [pl.BlockSpec(x_block, lambda i: (0, cur(i), 0)),
                   pl.BlockSpec(x_block, lambda i: (0, nxt(i), 0))]
        out_spec = pl.BlockSpec((N_SLABS, None, bsz * PLANE_PITCH, LANES),
                                lambda i: (0, done(i), 0, 0))
        out_shape = jax.ShapeDtypeStruct((N_SLABS, nmain, bsz * PLANE_PITCH, LANES), F32)
    plans = [_cast_plan(w.shape[0], nsteps) for w in cast]
    cast_specs = [pl.BlockSpec((chunk, w.shape[1]), lambda i, last=n - 1: (jnp.minimum(i, last), 0))
                  for w, (chunk, n) in zip(cast, plans)]
    outs = pl.pallas_call(
        functools.partial(_ffn_kernel, sublayer=sublayer, slab_in=slab_in, final_norm=final_norm,
                          cast_chunks=tuple(n for _, n in plans), nmain=nmain),
        grid=(nsteps,),
        in_specs=x_specs + [
            _resident((N_MOD_ROWS, bsz, d)),
            _resident((1, d)),
            _resident((d, 2 * f)),
            _resident((f, d)),
            _resident((1, d)),
        ] + cast_specs,
        out_specs=[out_spec] + cast_specs,
        out_shape=[out_shape] + [jax.ShapeDtypeStruct(w.shape, BF16) for w in cast],
        scratch_shapes=[
            pltpu.VMEM((tt * bsz, d), BF16),
            pltpu.VMEM((tt * bsz, d), BF16),
            pltpu.VMEM((tt * bsz, d), F32),
            pltpu.VMEM((tt * bsz, 1), F32),
        ],
        compiler_params=pltpu.CompilerParams(
            dimension_semantics=("arbitrary",), vmem_limit_bytes=VMEM_LIMIT_BYTES),
        name=f"ffn{sublayer}",
    )(x, x, mod, g.reshape(1, d), w_in, w_out, g_final.reshape(1, d), *cast)
    return outs[0], tuple(outs[1:])


def _mixer_kernel(x_ref, mod_ref, g_ref, win_ref, pw_ref, pb_ref, ps_ref, wpu_ref,
                  btre_ref, btim_ref, a2re_ref, a2im_ref, ctre_ref, ctim_ref, catre_ref, catim_ref,
                  cbt_ref, dsk_ref, wglu_ref, bglu_ref, wsu_ref, wout_ref, o_ref,
                  hist_ref, state_ref, st_ref):
    step = pl.program_id(0)
    bsz, tt, d = SUBLANES, TILE_FRAMES, D_MODEL
    rows = tt * bsz
    npair = tt // SCAN_FRAMES
    prow = npair * bsz
    halo = HALO_FRAMES * bsz
    shift, scale, gate = (mod_ref[3 * MIX_SUBLAYER + k] for k in range(3))
    tile3 = lambda v: v.reshape(tt, bsz, v.shape[-1])
    flat = lambda v: v.reshape(rows, v.shape[-1])

    @pl.when(step == 0)
    def _():
        state_ref[...] = jnp.zeros_like(state_ref)
        hist_ref[0:halo, :] = jnp.zeros((halo, POOL_WIDTH), F32)

    x = jnp.concatenate(
        [jnp.concatenate([x_ref[s, pl.ds(t, bsz, stride=PLANE_PITCH), :] for t in range(tt)], axis=0)
         for s in range(N_SLABS)], axis=-1)
    gs = g_ref[...] * (1.0 + scale)
    hb = flat(tile3(_rms_scale(x)) * gs[None] + shift[None]).astype(BF16)
    u_pool = _dot(hb, win_ref[:, 0:POOL_WIDTH])
    u_ssm = _dot(hb, win_ref[:, POOL_WIDTH:POOL_WIDTH + SSM_WIDTH])
    g0 = POOL_WIDTH + SSM_WIDTH

    hist_ref[halo:halo + rows, :] = u_pool
    frame = step * tt + lax.shift_right_logical(
        lax.broadcasted_iota(jnp.int32, (rows, 1), 0), int(math.log2(bsz)))
    zs = []
    for gi, w in enumerate(POOL_WINDOWS):
        lo = gi * POOL_GROUP
        cur = u_pool[:, lo:lo + POOL_GROUP]
        tot = cur
        for j in range(1, w):
            tot = tot + hist_ref[halo - j * bsz:halo - j * bsz + rows, lo:lo + POOL_GROUP]
        cnt = jnp.minimum(frame + 1, w).astype(F32)
        z = (tot / cnt - cur).astype(BF16)
        pz = _dot(z, pw_ref[gi]) + pb_ref[:, lo:lo + POOL_GROUP]
        zs.append(pz * ps_ref[:, lo:lo + POOL_GROUP])
    hist_ref[0:halo, :] = hist_ref[rows:rows + halo, :]
    pooled = jnp.concatenate(zs, axis=-1).astype(BF16)
    merged = jax.nn.sigmoid(_dot(hb, win_ref[:, g0:g0 + d])) * _dot(pooled, wpu_ref[...])
    gl_ssm = jax.nn.sigmoid(_dot(hb, win_ref[:, g0 + d:g0 + 2 * d]))

    u4 = u_ssm.reshape(npair, SCAN_FRAMES, bsz, SSM_WIDTH)
    ue = u4[:, 0].reshape(prow, SSM_WIDTH).astype(BF16)
    uo = u4[:, 1].reshape(prow, SSM_WIDTH).astype(BF16)
    kw = btre_ref.shape[1] // SCAN_FRAMES
    st_ref[0:bsz, :] = state_ref[...]
    for j in range(btre_ref.shape[0]):
        k0 = (j * MXU_DIM // (SSM_STATE // SSM_GROUP)) // kw * kw
        uk = jnp.concatenate([ue[:, k0:k0 + kw], uo[:, k0:k0 + kw]], axis=1)
        st_ref[bsz:bsz + prow, j * MXU_DIM:(j + 1) * MXU_DIM] = _dot(uk, btre_ref[j])
        st_ref[bsz:bsz + prow, N_STATES + j * MXU_DIM:N_STATES + (j + 1) * MXU_DIM] = \
            _dot(uk, btim_ref[j])

    for c in range(N_STATES // SCAN_LANES):
        re = slice(c * SCAN_LANES, (c + 1) * SCAN_LANES)
        im = slice(N_STATES + c * SCAN_LANES, N_STATES + (c + 1) * SCAN_LANES)
        a_re, a_im = a2re_ref[:, re], a2im_ref[:, re]
        s_re, s_im = state_ref[:, re], state_ref[:, im]
        for k in range(npair):
            fr = slice((k + 1) * bsz, (k + 2) * bsz)
            n_re = a_re * s_re - a_im * s_im + st_ref[fr, re]
            n_im = a_re * s_im + a_im * s_re + st_ref[fr, im]
            st_ref[fr, re] = n_re
            st_ref[fr, im] = n_im
            s_re, s_im = n_re, n_im
        state_ref[:, re] = s_re
        state_ref[:, im] = s_im

    kc = ctre_ref.shape[1]
    yo, ye = [], []
    for j in range(ctre_ref.shape[0]):
        cre = slice(j * kc, (j + 1) * kc)
        cim = slice(N_STATES + j * kc, N_STATES + (j + 1) * kc)
        so_re = st_ref[bsz:bsz + prow, cre].astype(BF16)
        so_im = st_ref[bsz:bsz + prow, cim].astype(BF16)
        sp_re = st_ref[0:prow, cre].astype(BF16)
        sp_im = st_ref[0:prow, cim].astype(BF16)
        yo.append(_dot(so_re, ctre_ref[j]) - _dot(so_im, ctim_ref[j]))
        ye.append(_dot(sp_re, catre_ref[j]) - _dot(sp_im, catim_ref[j])
                  + _dot(ue[:, j * MXU_DIM:(j + 1) * MXU_DIM], cbt_ref[j]))
    yo = jnp.concatenate(yo, axis=-1).reshape(npair, 1, bsz, SSM_WIDTH)
    ye = jnp.concatenate(ye, axis=-1).reshape(npair, 1, bsz, SSM_WIDTH)
    y = jnp.concatenate([ye, yo], axis=1).reshape(rows, SSM_WIDTH) + dsk_ref[...] * u_ssm
    y = 0.5 * y * (1.0 + lax.erf(y * (1.0 / math.sqrt(2.0))))
    vg = _dot(y.astype(BF16), wglu_ref[...]) + bglu_ref[...]
    glu = vg[:, 0:SSM_WIDTH] * jax.nn.sigmoid(vg[:, SSM_WIDTH:2 * SSM_WIDTH])
    merged = merged + gl_ssm * _dot(glu.astype(BF16), wsu_ref[...])
    mix = _dot(merged.astype(BF16), wout_ref[...])
    out = x + flat(gate[None] * tile3(mix))
    for s in range(N_SLABS):
        o_ref[s] = out[:, s * LANES:(s + 1) * LANES]


def _drive_tiles(bb):
    g, h, n = bb.shape
    gpt = MXU_DIM // n
    gpw = B_TILE_K // h
    tpw = gpw // gpt
    ntile = g // gpt
    bb5 = bb.reshape(ntile // tpw, tpw, gpt, h, n)
    place = (jnp.arange(gpw)[None, :, None]
             == gpt * jnp.arange(tpw)[:, None, None] + jnp.arange(gpt)[None, None, :])
    tiles = jnp.einsum("jpqhn,prq->jprhqn", bb5, place.astype(bb.dtype))
    return tiles.reshape(ntile, B_TILE_K, MXU_DIM).astype(BF16)


def _readout_tiles(cc):
    g, h, n = cc.shape
    nt = SSM_WIDTH // MXU_DIM
    gpt = g // nt
    eye = jnp.eye(gpt, dtype=cc.dtype)
    tiles = jnp.einsum("jghn,gk->jgnkh", cc.reshape(nt, gpt, h, n), eye)
    return tiles.reshape(nt, gpt * n, MXU_DIM).astype(BF16)


def _skip_tiles(cbt):
    g, k, h = cbt.shape
    nt = SSM_WIDTH // MXU_DIM
    gpt = g // nt
    eye = jnp.eye(gpt, dtype=cbt.dtype)
    tiles = jnp.einsum("jgkh,gq->jgkqh", cbt.reshape(nt, gpt, k, h), eye)
    return tiles.reshape(nt, MXU_DIM, MXU_DIM).astype(BF16)


def _mixer(x_planes, mod, g_mix, w_in, pool_w, pool_b, pool_scale, w_pool_up, ssm, c_re, c_im,
           d_skip, w_glu, b_glu, w_ssm_up, w_out, *, bsz, seq):
    d, tt = D_MODEL, TILE_FRAMES
    rows = tt * bsz
    a2_re, a2_im, bb_re, bb_im, abb_re, abb_im, ca_re, ca_im, cbt = ssm
    lane = lambda a: jnp.broadcast_to(a.reshape(1, N_STATES), (bsz, N_STATES))
    pair_drive = lambda first, second: jnp.concatenate(
        [_drive_tiles(first), _drive_tiles(second)], axis=1)
    args = (
        x_planes, mod, g_mix.reshape(1, d), w_in, pool_w.astype(BF16),
        pool_b.reshape(1, POOL_WIDTH), pool_scale.reshape(1, POOL_WIDTH), w_pool_up,
        pair_drive(abb_re, bb_re), pair_drive(abb_im, bb_im), lane(a2_re), lane(a2_im),
        _readout_tiles(c_re), _readout_tiles(c_im), _readout_tiles(ca_re), _readout_tiles(ca_im),
        _skip_tiles(cbt), d_skip.reshape(1, SSM_WIDTH),
        w_glu, b_glu.reshape(1, 2 * SSM_WIDTH), w_ssm_up, w_out,
    )
    plane_spec = pl.BlockSpec((N_SLABS, None, bsz * PLANE_PITCH, LANES), lambda i: (0, i, 0, 0))
    slab_spec = pl.BlockSpec((N_SLABS, rows, LANES), lambda i: (0, i, 0))
    return pl.pallas_call(
        _mixer_kernel,
        grid=(seq // tt,),
        in_specs=[plane_spec] + [_resident(a.shape) for a in args[1:]],
        out_specs=slab_spec,
        out_shape=jax.ShapeDtypeStruct((N_SLABS, seq * bsz, LANES), F32),
        scratch_shapes=[
            pltpu.VMEM((rows + HALO_FRAMES * bsz, POOL_WIDTH), F32),
            pltpu.VMEM((bsz, 2 * N_STATES), F32),
            pltpu.VMEM((bsz + rows // SCAN_FRAMES, 2 * N_STATES), F32),
        ],
        compiler_params=pltpu.CompilerParams(
            dimension_semantics=("arbitrary",), vmem_limit_bytes=VMEM_LIMIT_BYTES),
        name="mixer",
    )(*args)


def kernel(x, c, w_ada, b_ada, g_ffn1, w_ffn1_in, w_ffn1_out, g_mix, w_in, pool_w, pool_b,
           pool_scale, w_pool_up, ssm_lam_re_log, ssm_lam_im, ssm_log_dt, ssm_b_re, ssm_b_im,
           ssm_c_re, ssm_c_im, ssm_d, w_glu, b_glu, w_ssm_up, w_out, g_ffn2, w_ffn2_in,
           w_ffn2_out, g_final):
    bsz, seq, d = x.shape
    assert (bsz, d) == (SUBLANES, D_MODEL) and seq % TILE_FRAMES == 0
    assert w_ada.shape[0] == 1 and w_ada.shape[2] == N_MOD_ROWS * d, "single-layer block"
    l = 0
    mod = _adaln(c, w_ada[l], b_ada[l])
    ssm = _ssm_prep(ssm_lam_re_log[l], ssm_lam_im[l], ssm_log_dt[l], ssm_b_re[l], ssm_b_im[l],
                    ssm_c_re[l], ssm_c_im[l])
    later = (w_in[l], w_pool_up[l], w_glu[l], w_ssm_up[l], w_out[l], w_ffn2_in[l], w_ffn2_out[l])
    x1, later_bf16 = _ffn(x, mod, g_ffn1[l], w_ffn1_in[l].astype(BF16), w_ffn1_out[l].astype(BF16),
                          g_final, sublayer=0, slab_in=False, final_norm=False, bsz=bsz, seq=seq,
                          cast=later)
    w_in_b, w_pool_up_b, w_glu_b, w_ssm_up_b, w_out_b, w_ffn2_in_b, w_ffn2_out_b = later_bf16
    x2 = _mixer(x1, mod, g_mix[l], w_in_b, pool_w[l], pool_b[l], pool_scale[l], w_pool_up_b,
                ssm, ssm_c_re[l], ssm_c_im[l], ssm_d[l],
                w_glu_b, b_glu[l], w_ssm_up_b, w_out_b, bsz=bsz, seq=seq)
    out, _ = _ffn(x2, mod, g_ffn2[l], w_ffn2_in_b, w_ffn2_out_b, g_final,
                  sublayer=2, slab_in=True, final_norm=True, bsz=bsz, seq=seq)
    return out
```

```python
import functools
import math

import jax
import jax.numpy as jnp
from jax import lax
from jax.experimental import pallas as pl
from jax.experimental.pallas import tpu as pltpu

D_MODEL = 1024
D_FF = 2816
N_MOD_ROWS = 9
MIX_SUBLAYER = 1
POOL_WIDTH = 512
POOL_WINDOWS = (2, 4, 8, 16)
POOL_GROUP = 128
SSM_WIDTH = 512
SSM_GROUP = 16
N_SSM_GROUPS = 32
SSM_STATE = 64
N_STATES = N_SSM_GROUPS * SSM_STATE
EPS = 1e-6

F32 = jnp.float32
BF16 = jnp.bfloat16

SUBLANES = 8
LANES = 128
MXU_DIM = 256
BF16_TILE_ROWS = 16
VMEM_LIMIT_BYTES = 60000 * 1024

N_SLABS = D_MODEL // LANES
TILE_FRAMES = 64
PLANE_PITCH = TILE_FRAMES + SUBLANES
FFN_FC = MXU_DIM
HALO_FRAMES = 16
SCAN_LANES = 512
SCAN_FRAMES = 2
B_TILE_K = LANES


def _resident(shape):
    nd = len(shape)
    return pl.BlockSpec(shape, lambda *_: (0,) * nd, pipeline_mode=pl.Buffered(1))


def _rms_scale(x):
    return x * lax.rsqrt(jnp.mean(x * x, axis=-1, keepdims=True) + EPS)


def _dot(a, b):
    return jnp.dot(a, b, preferred_element_type=F32)


def _adaln_kernel(c_ref, w_ref, b_ref, o_ref):
    c = c_ref[...]
    s = c * jax.nn.sigmoid(c)
    o_ref[...] = _dot(s.astype(BF16), w_ref[...].astype(BF16)) + b_ref[...]


def _adaln(c, w, b):
    bsz, d = c.shape
    n = w.shape[1]
    return pl.pallas_call(
        _adaln_kernel,
        grid=(n // d,),
        in_specs=[
            pl.BlockSpec((bsz, d), lambda j: (0, 0)),
            pl.BlockSpec((d, d), lambda j: (0, j)),
            pl.BlockSpec((1, d), lambda j: (0, j)),
        ],
        out_specs=pl.BlockSpec((None, bsz, d), lambda j: (j, 0, 0)),
        out_shape=jax.ShapeDtypeStruct((n // d, bsz, d), F32),
        compiler_params=pltpu.CompilerParams(dimension_semantics=("arbitrary",)),
        name="adaln",
    )(c, w, b.reshape(1, n))


def _ssm_prep_kernel(lre_ref, lim_ref, ldt_ref, bre_ref, bim_ref, cre_ref, cim_ref,
                     a2_ref, drive_ref, read_ref, cbt_ref):
    lr = -jnp.exp(lre_ref[...])
    li = lim_ref[...]
    dt = jnp.exp(ldt_ref[...])
    mag = jnp.exp(lr * dt)
    ang = li * dt
    ab_re = mag * jnp.cos(ang)
    ab_im = mag * jnp.sin(ang)
    num_re = ab_re - 1.0
    num_im = ab_im
    den = lr * lr + li * li
    f_re = (num_re * lr + num_im * li) / den
    f_im = (num_im * lr - num_re * li) / den
    br = bre_ref[...]
    bi = bim_ref[...]
    bb_re = f_re * br - f_im * bi
    bb_im = f_re * bi + f_im * br
    c_re = cre_ref[...]
    c_im = cim_ref[...]
    a2_ref[0] = ab_re * ab_re - ab_im * ab_im
    a2_ref[1] = 2.0 * ab_re * ab_im
    drive_ref[0, 0] = ab_re * bb_re - ab_im * bb_im
    drive_ref[0, 1] = bb_re
    drive_ref[1, 0] = ab_re * bb_im + ab_im * bb_re
    drive_ref[1, 1] = bb_im
    read_ref[0] = c_re
    read_ref[1] = c_im
    read_ref[2] = c_re * ab_re - c_im * ab_im
    read_ref[3] = c_re * ab_im + c_im * ab_re
    lane = lax.broadcasted_iota(jnp.int32, cbt_ref.shape, 2)
    cbt = jnp.zeros(cbt_ref.shape, F32)
    for h in range(cbt_ref.shape[2]):
        col = jnp.sum(c_re[:, h:h + 1, :] * bb_re - c_im[:, h:h + 1, :] * bb_im,
                      axis=-1, keepdims=True)
        cbt = jnp.where(lane == h, col, cbt)
    cbt_ref[...] = cbt


def _ssm_prep(lam_re_log, lam_im, log_dt, b_re, b_im, c_re, c_im):
    g, n, h = b_re.shape
    per_state = lambda a: a.reshape(g, 1, n)
    ldt = jnp.broadcast_to(log_dt[:, None, None], (g, 1, n))
    full = lambda shape: pl.BlockSpec(shape, lambda: (0,) * len(shape))
    out_shapes = [(2, g, 1, n), (2, 2, g, h, n), (4, g, h, n), (g, h, h)]
    return pl.pallas_call(
        _ssm_prep_kernel,
        in_specs=[full((g, 1, n))] * 3 + [full((g, h, n))] * 4,
        out_specs=[full(s) for s in out_shapes],
        out_shape=[jax.ShapeDtypeStruct(s, F32) for s in out_shapes],
        name="ssm_prep",
    )(per_state(lam_re_log), per_state(lam_im), ldt,
      b_re.transpose(0, 2, 1), b_im.transpose(0, 2, 1), c_re, c_im)


def _cast_plan(rows, nsteps):
    chunk = BF16_TILE_ROWS
    while rows % chunk or rows // chunk > nsteps:
        chunk += BF16_TILE_ROWS
    return chunk, rows // chunk


def _zero_tile(v):
    bits = pltpu.bitcast(v, jnp.uint32)
    acc = None
    for r in range(0, bits.shape[0], SUBLANES):
        for c in range(0, bits.shape[1], LANES):
            t = bits[r:r + SUBLANES, c:c + LANES]
            acc = t if acc is None else acc | t
    half = jnp.uint32(16)
    return lax.shift_right_logical(lax.shift_right_logical(acc, half), half).astype(F32)


def _ffn_kernel(x_ref, xn_ref, mod_ref, g_ref, win_ref, wout_ref, gfin_ref, *rest,
                sublayer, slab_in, final_norm, cast_chunks, nmain):
    ncast = len(cast_chunks)
    cast_in, o_ref, cast_out = rest[:ncast], rest[ncast], rest[ncast + 1:2 * ncast + 1]
    ha_ref, hb_ref, pre_ref, rs_ref = rest[2 * ncast + 1:]
    bsz, tt, fc = SUBLANES, TILE_FRAMES, FFN_FC
    rows = bsz * tt
    nc = D_FF // fc
    row = lambda b, k: mod_ref[3 * sublayer + k, b:b + 1, :]
    seq_rows = lambda b: slice(b * tt, (b + 1) * tt)
    step = pl.program_id(0)

    for src, dst, nchunks in zip(cast_in, cast_out, cast_chunks):
        @pl.when(step < nchunks)
        def _(src=src, dst=dst):
            dst[...] = src[...].astype(BF16)

    def load_x(ref, b, lo, hi):
        if slab_in:
            return jnp.concatenate(
                [ref[s, pl.ds(b, tt, stride=bsz), :] for s in range(lo // LANES, hi // LANES)],
                axis=-1)
        return ref[b, :, lo:hi]

    def normalise(ref, h_ref, b):
        gs = g_ref[...] * (1.0 + row(b, 1))
        h = (_rms_scale(load_x(ref, b, 0, D_MODEL)) * gs + row(b, 0)).astype(BF16)
        h_ref[seq_rows(b), :] = h
        return h

    def finish(b):
        y = pre_ref[seq_rows(b), :] * rs_ref[seq_rows(b), :] * gfin_ref[...]
        o_ref[b] = y
        return y

    @pl.when(step == 0)
    def _():
        for b in range(bsz):
            normalise(x_ref, ha_ref, b)
        if final_norm:
            pre_ref[...] = jnp.zeros_like(pre_ref)
            rs_ref[...] = jnp.zeros_like(rs_ref)

    if not slab_in:
        for s in range(N_SLABS):
            for b in range(bsz):
                o_ref[s, b * PLANE_PITCH + tt:(b + 1) * PLANE_PITCH, :] = \
                    jnp.zeros((PLANE_PITCH - tt, LANES), F32)

    def main(h_cur, h_nxt):
        hb = h_cur[...]
        gl = []
        for c in range(nc):
            a = _dot(hb, win_ref[:, c * fc:(c + 1) * fc])
            bb = _dot(hb, win_ref[:, D_FF + c * fc:D_FF + (c + 1) * fc])
            if c < bsz:
                z = _zero_tile(normalise(xn_ref, h_nxt, c))
                if final_norm:
                    z = z + _zero_tile(finish(c))
                z = jnp.concatenate([z] * (fc // LANES), axis=1)
                a = (a.reshape(rows // SUBLANES, SUBLANES, fc) + z[None]).reshape(rows, fc)
            gl.append((a * jax.nn.sigmoid(a) * bb).astype(BF16))
        glu = jnp.concatenate(gl, axis=-1)
        sumsq = [None] * bsz
        for n in range(D_MODEL // fc):
            lo = n * fc
            r = _dot(glu, wout_ref[:, lo:lo + fc])
            for b in range(bsz):
                out = load_x(x_ref, b, lo, lo + fc) + (0.5 * row(b, 2)[:, lo:lo + fc]) * r[seq_rows(b)]
                if final_norm:
                    pre_ref[seq_rows(b), lo:lo + fc] = out
                    ss = jnp.sum(out * out, axis=-1, keepdims=True)
                    sumsq[b] = ss if sumsq[b] is None else sumsq[b] + ss
                elif slab_in:
                    o_ref[b, :, lo:lo + fc] = out
                else:
                    for s in range(fc // LANES):
                        o_ref[lo // LANES + s, b * PLANE_PITCH:b * PLANE_PITCH + tt, :] = \
                            out[:, s * LANES:(s + 1) * LANES]
        if final_norm:
            for b in range(bsz):
                rs_ref[seq_rows(b), :] = lax.rsqrt(sumsq[b] * (1.0 / D_MODEL) + EPS)

    even = lax.rem(step, 2) == 0
    pl.when(jnp.logical_and(step < nmain, even))(lambda: main(ha_ref, hb_ref))
    pl.when(jnp.logical_and(step < nmain, jnp.logical_not(even)))(lambda: main(hb_ref, ha_ref))

    if final_norm:
        @pl.when(step == nmain)
        def _():
            for b in range(bsz):
                finish(b)


def _ffn(x, mod, g, w_in, w_out, g_final, *, sublayer, slab_in, final_norm, bsz, seq, cast=()):
    d, f, tt = D_MODEL, D_FF, TILE_FRAMES
    nmain = seq // tt
    lag = 1 if final_norm else 0
    nsteps = nmain + lag
    cur = lambda i: jnp.minimum(i, nmain - 1)
    nxt = lambda i: jnp.minimum(i + 1, nmain - 1)
    done = lambda i: jnp.maximum(i - lag, 0)
    if slab_in:
        x_block = (N_SLABS, tt * bsz, LANES)
        x_specs = [pl.BlockSpec(x_block, lambda i: (0, cur(i), 0)),
                   pl.BlockSpec(x_block, lambda i: (0, nxt(i), 0))]
        out_spec = pl.BlockSpec((bsz, tt, d), lambda i: (0, done(i), 0))
        out_shape = jax.ShapeDtypeStruct((bsz, seq, d), F32)
    else:
        x_block = (bsz, tt, d)
        x_specs = [pl.BlockSpec(x_block, lambda i: (0, cur(i), 0)),
                   pl.BlockSpec(x_block, lambda i: (0, nxt(i), 0))]
        out_spec = pl.BlockSpec((N_SLABS, None, bsz * PLANE_PITCH, LANES),
                                lambda i: (0, done(i), 0, 0))
        out_shape = jax.ShapeDtypeStruct((N_SLABS, nmain, bsz * PLANE_PITCH, LANES), F32)
    plans = [_cast_plan(w.shape[0], nsteps) for w in cast]
    cast_specs = [pl.BlockSpec((chunk, w.shape[1]), lambda i, last=n - 1: (jnp.minimum(i, last), 0))
                  for w, (chunk, n) in zip(cast, plans)]
    outs = pl.pallas_call(
        functools.partial(_ffn_kernel, sublayer=sublayer, slab_in=slab_in, final_norm=final_norm,
                          cast_chunks=tuple(n for _, n in plans), nmain=nmain),
        grid=(nsteps,),
        in_specs=x_specs + [
            _resident((N_MOD_ROWS, bsz, d)),
            _resident((1, d)),
            _resident((d, 2 * f)),
            _resident((f, d)),
            _resident((1, d)),
        ] + cast_specs,
        out_specs=[out_spec] + cast_specs,
        out_shape=[out_shape] + [jax.ShapeDtypeStruct(w.shape, BF16) for w in cast],
        scratch_shapes=[
            pltpu.VMEM((tt * bsz, d), BF16),
            pltpu.VMEM((tt * bsz, d), BF16),
            pltpu.VMEM((tt * bsz, d), F32),
            pltpu.VMEM((tt * bsz, 1), F32),
        ],
        compiler_params=pltpu.CompilerParams(
            dimension_semantics=("arbitrary",), vmem_limit_bytes=VMEM_LIMIT_BYTES),
        name=f"ffn{sublayer}",
    )(x, x, mod, g.reshape(1, d), w_in, w_out, g_final.reshape(1, d), *cast)
    return outs[0], tuple(outs[1:])


def _mixer_kernel(x_ref, mod_ref, g_ref, win_ref, pw_ref, pb_ref, ps_ref, wpu_ref,
                  drv_ref, a2_ref, rd_ref, cbt_ref, dsk_ref, wglu_ref, bglu_ref, wsu_ref, wout_ref,
                  o_ref, hist_ref, state_ref, st_ref):
    step = pl.program_id(0)
    bsz, tt, d = SUBLANES, TILE_FRAMES, D_MODEL
    rows = tt * bsz
    npair = tt // SCAN_FRAMES
    prow = npair * bsz
    halo = HALO_FRAMES * bsz
    shift, scale, gate = (mod_ref[3 * MIX_SUBLAYER + k] for k in range(3))
    tile3 = lambda v: v.reshape(tt, bsz, v.shape[-1])
    flat = lambda v: v.reshape(rows, v.shape[-1])

    @pl.when(step == 0)
    def _():
        state_ref[...] = jnp.zeros_like(state_ref)
        hist_ref[0:halo, :] = jnp.zeros((halo, POOL_WIDTH), F32)

    x = jnp.concatenate(
        [jnp.concatenate([x_ref[s, pl.ds(t, bsz, stride=PLANE_PITCH), :] for t in range(tt)], axis=0)
         for s in range(N_SLABS)], axis=-1)
    gs = g_ref[...] * (1.0 + scale)
    hb = flat(tile3(_rms_scale(x)) * gs[None] + shift[None]).astype(BF16)
    u_pool = _dot(hb, win_ref[:, 0:POOL_WIDTH])
    u_ssm = _dot(hb, win_ref[:, POOL_WIDTH:POOL_WIDTH + SSM_WIDTH])
    g0 = POOL_WIDTH + SSM_WIDTH

    hist_ref[halo:halo + rows, :] = u_pool
    frame = step * tt + lax.shift_right_logical(
        lax.broadcasted_iota(jnp.int32, (rows, 1), 0), int(math.log2(bsz)))
    zs = []
    for gi, w in enumerate(POOL_WINDOWS):
        lo = gi * POOL_GROUP
        cur = u_pool[:, lo:lo + POOL_GROUP]
        tot = cur
        for j in range(1, w):
            tot = tot + hist_ref[halo - j * bsz:halo - j * bsz + rows, lo:lo + POOL_GROUP]
        cnt = jnp.minimum(frame + 1, w).astype(F32)
        z = (tot / cnt - cur).astype(BF16)
        pz = _dot(z, pw_ref[gi]) + pb_ref[:, lo:lo + POOL_GROUP]
        zs.append(pz * ps_ref[:, lo:lo + POOL_GROUP])
    hist_ref[0:halo, :] = hist_ref[rows:rows + halo, :]
    pooled = jnp.concatenate(zs, axis=-1).astype(BF16)
    merged = jax.nn.sigmoid(_dot(hb, win_ref[:, g0:g0 + d])) * _dot(pooled, wpu_ref[...])
    gl_ssm = jax.nn.sigmoid(_dot(hb, win_ref[:, g0 + d:g0 + 2 * d]))

    u4 = u_ssm.reshape(npair, SCAN_FRAMES, bsz, SSM_WIDTH)
    ue = u4[:, 0].reshape(prow, SSM_WIDTH).astype(BF16)
    uo = u4[:, 1].reshape(prow, SSM_WIDTH).astype(BF16)
    kw = drv_ref.shape[2] // SCAN_FRAMES
    st_ref[0:bsz, :] = state_ref[...]
    for j in range(drv_ref.shape[1]):
        k0 = (j * MXU_DIM // (SSM_STATE // SSM_GROUP)) // kw * kw
        uk = jnp.concatenate([ue[:, k0:k0 + kw], uo[:, k0:k0 + kw]], axis=1)
        st_ref[bsz:bsz + prow, j * MXU_DIM:(j + 1) * MXU_DIM] = _dot(uk, drv_ref[0, j])
        st_ref[bsz:bsz + prow, N_STATES + j * MXU_DIM:N_STATES + (j + 1) * MXU_DIM] = \
            _dot(uk, drv_ref[1, j])

    for c in range(N_STATES // SCAN_LANES):
        re = slice(c * SCAN_LANES, (c + 1) * SCAN_LANES)
        im = slice(N_STATES + c * SCAN_LANES, N_STATES + (c + 1) * SCAN_LANES)
        a_re, a_im = a2_ref[0, :, re], a2_ref[1, :, re]
        s_re, s_im = state_ref[:, re], state_ref[:, im]
        for k in range(npair):
            fr = slice((k + 1) * bsz, (k + 2) * bsz)
            n_re = a_re * s_re - a_im * s_im + st_ref[fr, re]
            n_im = a_re * s_im + a_im * s_re + st_ref[fr, im]
            st_ref[fr, re] = n_re
            st_ref[fr, im] = n_im
            s_re, s_im = n_re, n_im
        state_ref[:, re] = s_re
        state_ref[:, im] = s_im

    kc = rd_ref.shape[2]
    yo, ye = [], []
    for j in range(rd_ref.shape[1]):
        cre = slice(j * kc, (j + 1) * kc)
        cim = slice(N_STATES + j * kc, N_STATES + (j + 1) * kc)
        so_re = st_ref[bsz:bsz + prow, cre].astype(BF16)
        so_im = st_ref[bsz:bsz + prow, cim].astype(BF16)
        sp_re = st_ref[0:prow, cre].astype(BF16)
        sp_im = st_ref[0:prow, cim].astype(BF16)
        yo.append(_dot(so_re, rd_ref[0, j]) - _dot(so_im, rd_ref[1, j]))
        ye.append(_dot(sp_re, rd_ref[2, j]) - _dot(sp_im, rd_ref[3, j])
                  + _dot(ue[:, j * MXU_DIM:(j + 1) * MXU_DIM], cbt_ref[j]))
    yo = jnp.concatenate(yo, axis=-1).reshape(npair, 1, bsz, SSM_WIDTH)
    ye = jnp.concatenate(ye, axis=-1).reshape(npair, 1, bsz, SSM_WIDTH)
    y = jnp.concatenate([ye, yo], axis=1).reshape(rows, SSM_WIDTH) + dsk_ref[...] * u_ssm
    y = 0.5 * y * (1.0 + lax.erf(y * (1.0 / math.sqrt(2.0))))
    vg = _dot(y.astype(BF16), wglu_ref[...]) + bglu_ref[...]
    glu = vg[:, 0:SSM_WIDTH] * jax.nn.sigmoid(vg[:, SSM_WIDTH:2 * SSM_WIDTH])
    merged = merged + gl_ssm * _dot(glu.astype(BF16), wsu_ref[...])
    mix = _dot(merged.astype(BF16), wout_ref[...])
    out = x + flat(gate[None] * tile3(mix))
    for s in range(N_SLABS):
        o_ref[s] = out[:, s * LANES:(s + 1) * LANES]


def _drive_tiles(drive):
    c, f, g, h, n = drive.shape
    gpt = MXU_DIM // n
    gpw = B_TILE_K // h
    tpw = gpw // gpt
    ntile = g // gpt
    place = (jnp.arange(gpw)[None, :, None]
             == gpt * jnp.arange(tpw)[:, None, None] + jnp.arange(gpt)[None, None, :])
    tiles = jnp.einsum("cfjpqhn,prq->cjpfrhqn",
                       drive.reshape(c, f, ntile // tpw, tpw, gpt, h, n), place.astype(drive.dtype))
    return tiles.reshape(c, ntile, f * B_TILE_K, MXU_DIM).astype(BF16)


def _readout_tiles(read):
    x, g, h, n = read.shape
    nt = SSM_WIDTH // MXU_DIM
    gpt = g // nt
    eye = jnp.eye(gpt, dtype=read.dtype)
    tiles = jnp.einsum("xjghn,gk->xjgnkh", read.reshape(x, nt, gpt, h, n), eye)
    return tiles.reshape(x, nt, gpt * n, MXU_DIM).astype(BF16)


def _skip_tiles(cbt):
    g, k, h = cbt.shape
    nt = SSM_WIDTH // MXU_DIM
    gpt = g // nt
    eye = jnp.eye(gpt, dtype=cbt.dtype)
    tiles = jnp.einsum("jgkh,gq->jgkqh", cbt.reshape(nt, gpt, k, h), eye)
    return tiles.reshape(nt, MXU_DIM, MXU_DIM).astype(BF16)


def _mixer(x_planes, mod, g_mix, w_in, pool_w, pool_b, pool_scale, w_pool_up, ssm,
           d_skip, w_glu, b_glu, w_ssm_up, w_out, *, bsz, seq):
    d, tt = D_MODEL, TILE_FRAMES
    rows = tt * bsz
    a2, drive, read, cbt = ssm
    args = (
        x_planes, mod, g_mix.reshape(1, d), w_in, pool_w.astype(BF16),
        pool_b.reshape(1, POOL_WIDTH), pool_scale.reshape(1, POOL_WIDTH), w_pool_up,
        _drive_tiles(drive), jnp.broadcast_to(a2.reshape(2, 1, N_STATES), (2, bsz, N_STATES)),
        _readout_tiles(read), _skip_tiles(cbt), d_skip.reshape(1, SSM_WIDTH),
        w_glu, b_glu.reshape(1, 2 * SSM_WIDTH), w_ssm_up, w_out,
    )
    plane_spec = pl.BlockSpec((N_SLABS, None, bsz * PLANE_PITCH, LANES), lambda i: (0, i, 0, 0))
    slab_spec = pl.BlockSpec((N_SLABS, rows, LANES), lambda i: (0, i, 0))
    return pl.pallas_call(
        _mixer_kernel,
        grid=(seq // tt,),
        in_specs=[plane_spec] + [_resident(a.shape) for a in args[1:]],
        out_specs=slab_spec,
        out_shape=jax.ShapeDtypeStruct((N_SLABS, seq * bsz, LANES), F32),
        scratch_shapes=[
            pltpu.VMEM((rows + HALO_FRAMES * bsz, POOL_WIDTH), F32),
            pltpu.VMEM((bsz, 2 * N_STATES), F32),
            pltpu.VMEM((bsz + rows // SCAN_FRAMES, 2 * N_STATES), F32),
        ],
        compiler_params=pltpu.CompilerParams(
            dimension_semantics=("arbitrary",), vmem_limit_bytes=VMEM_LIMIT_BYTES),
        name="mixer",
    )(*args)


def kernel(x, c, w_ada, b_ada, g_ffn1, w_ffn1_in, w_ffn1_out, g_mix, w_in, pool_w, pool_b,
           pool_scale, w_pool_up, ssm_lam_re_log, ssm_lam_im, ssm_log_dt, ssm_b_re, ssm_b_im,
           ssm_c_re, ssm_c_im, ssm_d, w_glu, b_glu, w_ssm_up, w_out, g_ffn2, w_ffn2_in,
           w_ffn2_out, g_final):
    bsz, seq, d = x.shape
    assert (bsz, d) == (SUBLANES, D_MODEL) and seq % TILE_FRAMES == 0
    assert w_ada.shape[0] == 1 and w_ada.shape[2] == N_MOD_ROWS * d, "single-layer block"
    l = 0
    mod = _adaln(c, w_ada[l], b_ada[l])
    ssm = _ssm_prep(ssm_lam_re_log[l], ssm_lam_im[l], ssm_log_dt[l], ssm_b_re[l], ssm_b_im[l],
                    ssm_c_re[l], ssm_c_im[l])
    later = (w_in[l], w_pool_up[l], w_glu[l], w_ssm_up[l], w_out[l], w_ffn2_in[l], w_ffn2_out[l])
    x1, later_bf16 = _ffn(x, mod, g_ffn1[l], w_ffn1_in[l].astype(BF16), w_ffn1_out[l].astype(BF16),
                          g_final, sublayer=0, slab_in=False, final_norm=False, bsz=bsz, seq=seq,
                          cast=later)
    w_in_b, w_pool_up_b, w_glu_b, w_ssm_up_b, w_out_b, w_ffn2_in_b, w_ffn2_out_b = later_bf16
    x2 = _mixer(x1, mod, g_mix[l], w_in_b, pool_w[l], pool_b[l], pool_scale[l], w_pool_up_b,
                ssm, ssm_d[l],
                w_glu_b, b_glu[l], w_ssm_up_b, w_out_b, bsz=bsz, seq=seq)
    out, _ = _ffn(x2, mod, g_ffn2[l], w_ffn2_in_b, w_ffn2_out_b, g_final,
                  sublayer=2, slab_in=True, final_norm=True, bsz=bsz, seq=seq)
    return out
```

```python
import functools
import math

import jax
import jax.numpy as jnp
from jax import lax
from jax.experimental import pallas as pl
from jax.experimental.pallas import tpu as pltpu

D_MODEL = 1024
D_FF = 2816
N_MOD_ROWS = 9
MIX_SUBLAYER = 1
POOL_WIDTH = 512
POOL_WINDOWS = (2, 4, 8, 16)
POOL_GROUP = 128
SSM_WIDTH = 512
SSM_GROUP = 16
N_SSM_GROUPS = 32
SSM_STATE = 64
N_STATES = N_SSM_GROUPS * SSM_STATE
EPS = 1e-6

F32 = jnp.float32
BF16 = jnp.bfloat16

SUBLANES = 8
LANES = 128
MXU_DIM = 256
BF16_TILE_ROWS = 16
VMEM_LIMIT_BYTES = 60000 * 1024

N_SLABS = D_MODEL // LANES
TILE_FRAMES = 64
PLANE_PITCH = TILE_FRAMES + SUBLANES
FFN_FC = MXU_DIM
HALO_FRAMES = 16
SCAN_LANES = 512
SCAN_FRAMES = 2
B_TILE_K = LANES


def _resident(shape):
    nd = len(shape)
    return pl.BlockSpec(shape, lambda *_: (0,) * nd, pipeline_mode=pl.Buffered(1))


def _rms_scale(x):
    return x * lax.rsqrt(jnp.mean(x * x, axis=-1, keepdims=True) + EPS)


def _dot(a, b):
    return jnp.dot(a, b, preferred_element_type=F32)


def _adaln_kernel(c_ref, w_ref, b_ref, o_ref):
    c = c_ref[...]
    s = c * jax.nn.sigmoid(c)
    o_ref[...] = _dot(s.astype(BF16), w_ref[...].astype(BF16)) + b_ref[...]


def _adaln(c, w, b):
    bsz, d = c.shape
    n = w.shape[1]
    return pl.pallas_call(
        _adaln_kernel,
        grid=(n // d,),
        in_specs=[
            pl.BlockSpec((bsz, d), lambda j: (0, 0)),
            pl.BlockSpec((d, d), lambda j: (0, j)),
            pl.BlockSpec((1, d), lambda j: (0, j)),
        ],
        out_specs=pl.BlockSpec((None, bsz, d), lambda j: (j, 0, 0)),
        out_shape=jax.ShapeDtypeStruct((n // d, bsz, d), F32),
        compiler_params=pltpu.CompilerParams(dimension_semantics=("arbitrary",)),
        name="adaln",
    )(c, w, b.reshape(1, n))


def _ssm_prep_kernel(lre_ref, lim_ref, ldt_ref, bre_ref, bim_ref, cre_ref, cim_ref,
                     a2_ref, drive_ref, read_ref, cbt_ref):
    lr = -jnp.exp(lre_ref[...])
    li = lim_ref[...]
    dt = jnp.exp(ldt_ref[...])
    mag = jnp.exp(lr * dt)
    ang = li * dt
    ab_re = mag * jnp.cos(ang)
    ab_im = mag * jnp.sin(ang)
    num_re = ab_re - 1.0
    num_im = ab_im
    den = lr * lr + li * li
    f_re = (num_re * lr + num_im * li) / den
    f_im = (num_im * lr - num_re * li) / den
    br = bre_ref[...]
    bi = bim_ref[...]
    bb_re = f_re * br - f_im * bi
    bb_im = f_re * bi + f_im * br
    c_re = cre_ref[...]
    c_im = cim_ref[...]
    a2_ref[0] = ab_re * ab_re - ab_im * ab_im
    a2_ref[1] = 2.0 * ab_re * ab_im
    drive_ref[0, 0] = ab_re * bb_re - ab_im * bb_im
    drive_ref[0, 1] = bb_re
    drive_ref[1, 0] = ab_re * bb_im + ab_im * bb_re
    drive_ref[1, 1] = bb_im
    read_ref[0] = c_re
    read_ref[1] = c_im
    read_ref[2] = c_re * ab_re - c_im * ab_im
    read_ref[3] = c_re * ab_im + c_im * ab_re
    lane = lax.broadcasted_iota(jnp.int32, cbt_ref.shape, 2)
    cbt = jnp.zeros(cbt_ref.shape, F32)
    for h in range(cbt_ref.shape[2]):
        col = jnp.sum(c_re[:, h:h + 1, :] * bb_re - c_im[:, h:h + 1, :] * bb_im,
                      axis=-1, keepdims=True)
        cbt = jnp.where(lane == h, col, cbt)
    cbt_ref[...] = cbt


def _ssm_prep(lam_re_log, lam_im, log_dt, b_re, b_im, c_re, c_im):
    g, n, h = b_re.shape
    per_state = lambda a: a.reshape(g, 1, n)
    ldt = jnp.broadcast_to(log_dt[:, None, None], (g, 1, n))
    full = lambda shape: pl.BlockSpec(shape, lambda: (0,) * len(shape))
    out_shapes = [(2, g, 1, n), (2, 2, g, h, n), (4, g, h, n), (g, h, h)]
    return pl.pallas_call(
        _ssm_prep_kernel,
        in_specs=[full((g, 1, n))] * 3 + [full((g, h, n))] * 4,
        out_specs=[full(s) for s in out_shapes],
        out_shape=[jax.ShapeDtypeStruct(s, F32) for s in out_shapes],
        name="ssm_prep",
    )(per_state(lam_re_log), per_state(lam_im), ldt,
      b_re.transpose(0, 2, 1), b_im.transpose(0, 2, 1), c_re, c_im)


def _cast_plan(rows, nsteps):
    chunk = BF16_TILE_ROWS
    while rows % chunk or rows // chunk > nsteps:
        chunk += BF16_TILE_ROWS
    return chunk, rows // chunk


def _zero_tile(v):
    bits = pltpu.bitcast(v, jnp.uint32)
    acc = None
    for r in range(0, bits.shape[0], SUBLANES):
        for c in range(0, bits.shape[1], LANES):
            t = bits[r:r + SUBLANES, c:c + LANES]
            acc = t if acc is None else acc | t
    half = jnp.uint32(16)
    return lax.shift_right_logical(lax.shift_right_logical(acc, half), half).astype(F32)


def _ffn_kernel(x_ref, xn_ref, mod_ref, g_ref, win_ref, wout_ref, gfin_ref, *rest,
                sublayer, slab_in, final_norm, cast_chunks, nmain):
    ncast = len(cast_chunks)
    cast_in, o_ref, cast_out = rest[:ncast], rest[ncast], rest[ncast + 1:2 * ncast + 1]
    ha_ref, hb_ref, pre_ref, rs_ref = rest[2 * ncast + 1:]
    bsz, tt, fc = SUBLANES, TILE_FRAMES, FFN_FC
    rows = bsz * tt
    nc = D_FF // fc
    row = lambda b, k: mod_ref[3 * sublayer + k, b:b + 1, :]
    seq_rows = lambda b: slice(b * tt, (b + 1) * tt)
    step = pl.program_id(0)

    for src, dst, nchunks in zip(cast_in, cast_out, cast_chunks):
        @pl.when(step < nchunks)
        def _(src=src, dst=dst):
            dst[...] = src[...].astype(BF16)

    def load_x(ref, b, lo, hi):
        if slab_in:
            return jnp.concatenate(
                [ref[s, pl.ds(b, tt, stride=bsz), :] for s in range(lo // LANES, hi // LANES)],
                axis=-1)
        return ref[b, :, lo:hi]

    def normalise(ref, h_ref, b):
        gs = g_ref[...] * (1.0 + row(b, 1))
        h = (_rms_scale(load_x(ref, b, 0, D_MODEL)) * gs + row(b, 0)).astype(BF16)
        h_ref[seq_rows(b), :] = h
        return h

    def finish(b):
        y = pre_ref[seq_rows(b), :] * rs_ref[seq_rows(b), :] * gfin_ref[...]
        o_ref[b] = y
        return y

    @pl.when(step == 0)
    def _():
        for b in range(bsz):
            normalise(x_ref, ha_ref, b)
        if final_norm:
            pre_ref[...] = jnp.zeros_like(pre_ref)
            rs_ref[...] = jnp.zeros_like(rs_ref)

    if not slab_in:
        for s in range(N_SLABS):
            for b in range(bsz):
                o_ref[s, b * PLANE_PITCH + tt:(b + 1) * PLANE_PITCH, :] = \
                    jnp.zeros((PLANE_PITCH - tt, LANES), F32)

    def main(h_cur, h_nxt):
        hb = h_cur[...]
        gl = []
        for c in range(nc):
            a = _dot(hb, win_ref[:, c * fc:(c + 1) * fc])
            bb = _dot(hb, win_ref[:, D_FF + c * fc:D_FF + (c + 1) * fc])
            if c < bsz:
                z = _zero_tile(normalise(xn_ref, h_nxt, c))
                if final_norm:
                    z = z + _zero_tile(finish(c))
                z = jnp.concatenate([z] * (fc // LANES), axis=1)
                a = (a.reshape(rows // SUBLANES, SUBLANES, fc) + z[None]).reshape(rows, fc)
            gl.append((a * jax.nn.sigmoid(a) * bb).astype(BF16))
        glu = jnp.concatenate(gl, axis=-1)
        sumsq = [None] * bsz
        for n in range(D_MODEL // fc):
            lo = n * fc
            r = _dot(glu, wout_ref[:, lo:lo + fc])
            for b in range(bsz):
                out = load_x(x_ref, b, lo, lo + fc) + (0.5 * row(b, 2)[:, lo:lo + fc]) * r[seq_rows(b)]
                if final_norm:
                    pre_ref[seq_rows(b), lo:lo + fc] = out
                    ss = jnp.sum(out * out, axis=-1, keepdims=True)
                    sumsq[b] = ss if sumsq[b] is None else sumsq[b] + ss
                elif slab_in:
                    o_ref[b, :, lo:lo + fc] = out
                else:
                    for s in range(fc // LANES):
                        o_ref[lo // LANES + s, b * PLANE_PITCH:b * PLANE_PITCH + tt, :] = \
                            out[:, s * LANES:(s + 1) * LANES]
        if final_norm:
            for b in range(bsz):
                rs_ref[seq_rows(b), :] = lax.rsqrt(sumsq[b] * (1.0 / D_MODEL) + EPS)

    even = lax.rem(step, 2) == 0
    pl.when(jnp.logical_and(step < nmain, even))(lambda: main(ha_ref, hb_ref))
    pl.when(jnp.logical_and(step < nmain, jnp.logical_not(even)))(lambda: main(hb_ref, ha_ref))

    if final_norm:
        @pl.when(step == nmain)
        def _():
            for b in range(bsz):
                finish(b)


def _ffn(x, mod, g, w_in, w_out, g_final, *, sublayer, slab_in, final_norm, bsz, seq, cast=()):
    d, f, tt = D_MODEL, D_FF, TILE_FRAMES
    nmain = seq // tt
    lag = 1 if final_norm else 0
    nsteps = nmain + lag
    cur = lambda i: jnp.minimum(i, nmain - 1)
    nxt = lambda i: jnp.minimum(i + 1, nmain - 1)
    done = lambda i: jnp.maximum(i - lag, 0)
    if slab_in:
        x_block = (N_SLABS, tt * bsz, LANES)
        x_specs = [pl.BlockSpec(x_block, lambda i: (0, cur(i), 0)),
                   pl.BlockSpec(x_block, lambda i: (0, nxt(i), 0))]
        out_spec = pl.BlockSpec((bsz, tt, d), lambda i: (0, done(i), 0))
        out_shape = jax.ShapeDtypeStruct((bsz, seq, d), F32)
    else:
        x_block = (bsz, tt, d)
        x_specs = [pl.BlockSpec(x_block, lambda i: (0, cur(i), 0)),
                   pl.BlockSpec(x_block, lambda i: (0, nxt(i), 0))]
        out_spec = pl.BlockSpec((N_SLABS, None, bsz * PLANE_PITCH, LANES),
                                lambda i: (0, done(i), 0, 0))
        out_shape = jax.ShapeDtypeStruct((N_SLABS, nmain, bsz * PLANE_PITCH, LANES), F32)
    plans = [_cast_plan(w.shape[0], nsteps) for w in cast]
    cast_specs = [pl.BlockSpec((chunk, w.shape[1]), lambda i, last=n - 1: (jnp.minimum(i, last), 0))
                  for w, (chunk, n) in zip(cast, plans)]
    outs = pl.pallas_call(
        functools.partial(_ffn_kernel, sublayer=sublayer, slab_in=slab_in, final_norm=final_norm,
                          cast_chunks=tuple(n for _, n in plans), nmain=nmain),
        grid=(nsteps,),
        in_specs=x_specs + [
            _resident((N_MOD_ROWS, bsz, d)),
            _resident((1, d)),
            _resident((d, 2 * f)),
            _resident((f, d)),
            _resident((1, d)),
        ] + cast_specs,
        out_specs=[out_spec] + cast_specs,
        out_shape=[out_shape] + [jax.ShapeDtypeStruct(w.shape, BF16) for w in cast],
        scratch_shapes=[
            pltpu.VMEM((tt * bsz, d), BF16),
            pltpu.VMEM((tt * bsz, d), BF16),
            pltpu.VMEM((tt * bsz, d), F32),
            pltpu.VMEM((tt * bsz, 1), F32),
        ],
        compiler_params=pltpu.CompilerParams(
            dimension_semantics=("arbitrary",), vmem_limit_bytes=VMEM_LIMIT_BYTES),
        name=f"ffn{sublayer}",
    )(x, x, mod, g.reshape(1, d), w_in, w_out, g_final.reshape(1, d), *cast)
    return outs[0], tuple(outs[1:])


def _mixer_kernel(x_ref, mod_ref, g_ref, win_ref, pw_ref, pb_ref, ps_ref, wpu_ref,
                  drv_ref, a2_ref, rd_ref, cbt_ref, dsk_ref, wglu_ref, bglu_ref, wsu_ref, wout_ref,
                  o_ref, hist_ref, state_ref, st_ref):
    step = pl.program_id(0)
    bsz, tt, d = SUBLANES, TILE_FRAMES, D_MODEL
    rows = tt * bsz
    npair = tt // SCAN_FRAMES
    prow = npair * bsz
    halo = HALO_FRAMES * bsz
    shift, scale, gate = (mod_ref[3 * MIX_SUBLAYER + k] for k in range(3))
    tile3 = lambda v: v.reshape(tt, bsz, v.shape[-1])
    flat = lambda v: v.reshape(rows, v.shape[-1])

    @pl.when(step == 0)
    def _():
        state_ref[...] = jnp.zeros_like(state_ref)
        hist_ref[0:halo, :] = jnp.zeros((halo, POOL_WIDTH), F32)

    x = jnp.concatenate(
        [jnp.concatenate([x_ref[s, pl.ds(t, bsz, stride=PLANE_PITCH), :] for t in range(tt)], axis=0)
         for s in range(N_SLABS)], axis=-1)
    gs = g_ref[...] * (1.0 + scale)
    hb = flat(tile3(_rms_scale(x)) * gs[None] + shift[None]).astype(BF16)
    u_pool = _dot(hb, win_ref[:, 0:POOL_WIDTH])
    u_ssm = _dot(hb, win_ref[:, POOL_WIDTH:POOL_WIDTH + SSM_WIDTH])
    g0 = POOL_WIDTH + SSM_WIDTH

    hist_ref[halo:halo + rows, :] = u_pool
    frame = step * tt + lax.shift_right_logical(
        lax.broadcasted_iota(jnp.int32, (rows, 1), 0), int(math.log2(bsz)))
    zs = []
    for gi, w in enumerate(POOL_WINDOWS):
        lo = gi * POOL_GROUP
        cur = u_pool[:, lo:lo + POOL_GROUP]
        tot = cur
        for j in range(1, w):
            tot = tot + hist_ref[halo - j * bsz:halo - j * bsz + rows, lo:lo + POOL_GROUP]
        cnt = jnp.minimum(frame + 1, w).astype(F32)
        z = (tot / cnt - cur).astype(BF16)
        pz = _dot(z, pw_ref[gi]) + pb_ref[:, lo:lo + POOL_GROUP]
        zs.append(pz * ps_ref[:, lo:lo + POOL_GROUP])
    hist_ref[0:halo, :] = hist_ref[rows:rows + halo, :]
    pooled = jnp.concatenate(zs, axis=-1).astype(BF16)
    merged = jax.nn.sigmoid(_dot(hb, win_ref[:, g0:g0 + d])) * _dot(pooled, wpu_ref[...])
    gl_ssm = jax.nn.sigmoid(_dot(hb, win_ref[:, g0 + d:g0 + 2 * d]))

    u4 = u_ssm.reshape(npair, SCAN_FRAMES, bsz, SSM_WIDTH)
    ue = u4[:, 0].reshape(prow, SSM_WIDTH).astype(BF16)
    uo = u4[:, 1].reshape(prow, SSM_WIDTH).astype(BF16)
    kw = drv_ref.shape[2] // SCAN_FRAMES
    st_ref[0:bsz, :] = state_ref[...]
    for j in range(drv_ref.shape[1]):
        k0 = (j * MXU_DIM // (SSM_STATE // SSM_GROUP)) // kw * kw
        uk = jnp.concatenate([ue[:, k0:k0 + kw], uo[:, k0:k0 + kw]], axis=1)
        st_ref[bsz:bsz + prow, j * MXU_DIM:(j + 1) * MXU_DIM] = _dot(uk, drv_ref[0, j])
        st_ref[bsz:bsz + prow, N_STATES + j * MXU_DIM:N_STATES + (j + 1) * MXU_DIM] = \
            _dot(uk, drv_ref[1, j])

    for c in range(N_STATES // SCAN_LANES):
        re = slice(c * SCAN_LANES, (c + 1) * SCAN_LANES)
        im = slice(N_STATES + c * SCAN_LANES, N_STATES + (c + 1) * SCAN_LANES)
        a_re, a_im = a2_ref[0, :, re], a2_ref[1, :, re]
        s_re, s_im = state_ref[:, re], state_ref[:, im]
        for k in range(npair):
            fr = slice((k + 1) * bsz, (k + 2) * bsz)
            n_re = a_re * s_re - a_im * s_im + st_ref[fr, re]
            n_im = a_re * s_im + a_im * s_re + st_ref[fr, im]
            st_ref[fr, re] = n_re
            st_ref[fr, im] = n_im
            s_re, s_im = n_re, n_im
        state_ref[:, re] = s_re
        state_ref[:, im] = s_im

    kc = rd_ref.shape[2]
    yo, ye = [], []
    for j in range(rd_ref.shape[1]):
        cre = slice(j * kc, (j + 1) * kc)
        cim = slice(N_STATES + j * kc, N_STATES + (j + 1) * kc)
        so_re = st_ref[bsz:bsz + prow, cre].astype(BF16)
        so_im = st_ref[bsz:bsz + prow, cim].astype(BF16)
        sp_re = st_ref[0:prow, cre].astype(BF16)
        sp_im = st_ref[0:prow, cim].astype(BF16)
        yo.append(_dot(so_re, rd_ref[0, j]) - _dot(so_im, rd_ref[1, j]))
        ye.append(_dot(sp_re, rd_ref[2, j]) - _dot(sp_im, rd_ref[3, j])
                  + _dot(ue[:, j * MXU_DIM:(j + 1) * MXU_DIM], cbt_ref[j]))
    yo = jnp.concatenate(yo, axis=-1).reshape(npair, 1, bsz, SSM_WIDTH)
    ye = jnp.concatenate(ye, axis=-1).reshape(npair, 1, bsz, SSM_WIDTH)
    y = jnp.concatenate([ye, yo], axis=1).reshape(rows, SSM_WIDTH) + dsk_ref[...] * u_ssm
    y = 0.5 * y * (1.0 + lax.erf(y * (1.0 / math.sqrt(2.0))))
    vg = _dot(y.astype(BF16), wglu_ref[...]) + bglu_ref[...]
    glu = vg[:, 0:SSM_WIDTH] * jax.nn.sigmoid(vg[:, SSM_WIDTH:2 * SSM_WIDTH])
    merged = merged + gl_ssm * _dot(glu.astype(BF16), wsu_ref[...])
    mix = _dot(merged.astype(BF16), wout_ref[...])
    out = x + flat(gate[None] * tile3(mix))
    for s in range(N_SLABS):
        o_ref[s] = out[:, s * LANES:(s + 1) * LANES]


def _drive_tiles(drive):
    c, f, g, h, n = drive.shape
    gpt = MXU_DIM // n
    gpw = B_TILE_K // h
    tpw = gpw // gpt
    ntile = g // gpt
    wide = drive.reshape(c, f, ntile, gpt, h, n).transpose(0, 2, 1, 4, 3, 5)
    wide = wide.reshape(c, ntile, f, 1, h, gpt * n)
    slot = jnp.arange(gpw)[None, :, None, None]
    own = (gpt * (jnp.arange(ntile) % tpw)[:, None, None, None]
           + (jnp.arange(gpt * n) // n)[None, None, None, :])
    tiles = jnp.where((slot == own)[None, :, None], wide, 0)
    return tiles.reshape(c, ntile, f * B_TILE_K, MXU_DIM).astype(BF16)


def _readout_tiles(read):
    x, g, h, n = read.shape
    nt = SSM_WIDTH // MXU_DIM
    gpt = g // nt
    rows = read.reshape(x, nt, gpt, h, n).transpose(0, 1, 2, 4, 3).reshape(x, nt, gpt * n, h)
    return _spread_blocks(rows, n)


def _spread_blocks(rows, rows_per_block):
    h = rows.shape[-1]
    cols = jnp.arange(MXU_DIM)[None, :]
    copy = (jnp.arange(h)[:, None] == cols % h).astype(BF16)
    wide = jnp.einsum("...rh,hc->...rc", rows.astype(BF16), copy, preferred_element_type=F32)
    keep = jnp.arange(rows.shape[-2])[:, None] // rows_per_block == cols // h
    return jnp.where(keep, wide, 0).astype(BF16)


def _skip_tiles(cbt):
    g, k, h = cbt.shape
    nt = SSM_WIDTH // MXU_DIM
    return _spread_blocks(cbt.reshape(nt, g // nt * k, h), k)


def _mixer(x_planes, mod, g_mix, w_in, pool_w, pool_b, pool_scale, w_pool_up, ssm,
           d_skip, w_glu, b_glu, w_ssm_up, w_out, *, bsz, seq):
    d, tt = D_MODEL, TILE_FRAMES
    rows = tt * bsz
    a2, drive, read, cbt = ssm
    args = (
        x_planes, mod, g_mix.reshape(1, d), w_in, pool_w.astype(BF16),
        pool_b.reshape(1, POOL_WIDTH), pool_scale.reshape(1, POOL_WIDTH), w_pool_up,
        _drive_tiles(drive), jnp.broadcast_to(a2.reshape(2, 1, N_STATES), (2, bsz, N_STATES)),
        _readout_tiles(read), _skip_tiles(cbt), d_skip.reshape(1, SSM_WIDTH),
        w_glu, b_glu.reshape(1, 2 * SSM_WIDTH), w_ssm_up, w_out,
    )
    plane_spec = pl.BlockSpec((N_SLABS, None, bsz * PLANE_PITCH, LANES), lambda i: (0, i, 0, 0))
    slab_spec = pl.BlockSpec((N_SLABS, rows, LANES), lambda i: (0, i, 0))
    return pl.pallas_call(
        _mixer_kernel,
        grid=(seq // tt,),
        in_specs=[plane_spec] + [_resident(a.shape) for a in args[1:]],
        out_specs=slab_spec,
        out_shape=jax.ShapeDtypeStruct((N_SLABS, seq * bsz, LANES), F32),
        scratch_shapes=[
            pltpu.VMEM((rows + HALO_FRAMES * bsz, POOL_WIDTH), F32),
            pltpu.VMEM((bsz, 2 * N_STATES), F32),
            pltpu.VMEM((bsz + rows // SCAN_FRAMES, 2 * N_STATES), F32),
        ],
        compiler_params=pltpu.CompilerParams(
            dimension_semantics=("arbitrary",), vmem_limit_bytes=VMEM_LIMIT_BYTES),
        name="mixer",
    )(*args)


def kernel(x, c, w_ada, b_ada, g_ffn1, w_ffn1_in, w_ffn1_out, g_mix, w_in, pool_w, pool_b,
           pool_scale, w_pool_up, ssm_lam_re_log, ssm_lam_im, ssm_log_dt, ssm_b_re, ssm_b_im,
           ssm_c_re, ssm_c_im, ssm_d, w_glu, b_glu, w_ssm_up, w_out, g_ffn2, w_ffn2_in,
           w_ffn2_out, g_final):
    bsz, seq, d = x.shape
    assert (bsz, d) == (SUBLANES, D_MODEL) and seq % TILE_FRAMES == 0
    assert w_ada.shape[0] == 1 and w_ada.shape[2] == N_MOD_ROWS * d, "single-layer block"
    l = 0
    mod = _adaln(c, w_ada[l], b_ada[l])
    ssm = _ssm_prep(ssm_lam_re_log[l], ssm_lam_im[l], ssm_log_dt[l], ssm_b_re[l], ssm_b_im[l],
                    ssm_c_re[l], ssm_c_im[l])
    later = (w_in[l], w_pool_up[l], w_glu[l], w_ssm_up[l], w_out[l], w_ffn2_in[l], w_ffn2_out[l])
    x1, later_bf16 = _ffn(x, mod, g_ffn1[l], w_ffn1_in[l].astype(BF16), w_ffn1_out[l].astype(BF16),
                          g_final, sublayer=0, slab_in=False, final_norm=False, bsz=bsz, seq=seq,
                          cast=later)
    w_in_b, w_pool_up_b, w_glu_b, w_ssm_up_b, w_out_b, w_ffn2_in_b, w_ffn2_out_b = later_bf16
    x2 = _mixer(x1, mod, g_mix[l], w_in_b, pool_w[l], pool_b[l], pool_scale[l], w_pool_up_b,
                ssm, ssm_d[l],
                w_glu_b, b_glu[l], w_ssm_up_b, w_out_b, bsz=bsz, seq=seq)
    out, _ = _ffn(x2, mod, g_ffn2[l], w_ffn2_in_b, w_ffn2_out_b, g_final,
                  sublayer=2, slab_in=True, final_norm=True, bsz=bsz, seq=seq)
    return out
```

```python
import functools
import math

import jax
import jax.numpy as jnp
from jax import lax
from jax.experimental import pallas as pl
from jax.experimental.pallas import tpu as pltpu

D_MODEL = 1024
D_FF = 2816
N_MOD_ROWS = 9
MIX_SUBLAYER = 1
POOL_WIDTH = 512
POOL_WINDOWS = (2, 4, 8, 16)
POOL_GROUP = 128
SSM_WIDTH = 512
SSM_GROUP = 16
N_SSM_GROUPS = 32
SSM_STATE = 64
N_STATES = N_SSM_GROUPS * SSM_STATE
EPS = 1e-6

F32 = jnp.float32
BF16 = jnp.bfloat16

SUBLANES = 8
LANES = 128
MXU_DIM = 256
BF16_TILE_ROWS = 16
VMEM_LIMIT_BYTES = 60000 * 1024

N_SLABS = D_MODEL // LANES
TILE_FRAMES = 64
PLANE_PITCH = TILE_FRAMES + SUBLANES
FFN_FC = MXU_DIM
HALO_FRAMES = 16
SCAN_LANES = 512
SCAN_FRAMES = 2
B_TILE_K = LANES


def _resident(shape):
    nd = len(shape)
    return pl.BlockSpec(shape, lambda *_: (0,) * nd, pipeline_mode=pl.Buffered(1))


def _rms_scale(x):
    return x * lax.rsqrt(jnp.mean(x * x, axis=-1, keepdims=True) + EPS)


def _dot(a, b):
    return jnp.dot(a, b, preferred_element_type=F32)


def _adaln_kernel(c_ref, w_ref, b_ref, o_ref):
    c = c_ref[...]
    s = c * jax.nn.sigmoid(c)
    o_ref[...] = _dot(s.astype(BF16), w_ref[...].astype(BF16)) + b_ref[...]


def _adaln(c, w, b):
    bsz, d = c.shape
    n = w.shape[1]
    return pl.pallas_call(
        _adaln_kernel,
        grid=(n // d,),
        in_specs=[
            pl.BlockSpec((bsz, d), lambda j: (0, 0)),
            pl.BlockSpec((d, d), lambda j: (0, j)),
            pl.BlockSpec((1, d), lambda j: (0, j)),
        ],
        out_specs=pl.BlockSpec((None, bsz, d), lambda j: (j, 0, 0)),
        out_shape=jax.ShapeDtypeStruct((n // d, bsz, d), F32),
        compiler_params=pltpu.CompilerParams(dimension_semantics=("arbitrary",)),
        name="adaln",
    )(c, w, b.reshape(1, n))


def _ssm_prep_kernel(lre_ref, lim_ref, ldt_ref, bre_ref, bim_ref, cre_ref, cim_ref,
                     a2_ref, drive_ref, read_ref, cbt_ref):
    lr = -jnp.exp(lre_ref[...])
    li = lim_ref[...]
    dt = jnp.exp(ldt_ref[...])
    mag = jnp.exp(lr * dt)
    ang = li * dt
    ab_re = mag * jnp.cos(ang)
    ab_im = mag * jnp.sin(ang)
    num_re = ab_re - 1.0
    num_im = ab_im
    den = lr * lr + li * li
    f_re = (num_re * lr + num_im * li) / den
    f_im = (num_im * lr - num_re * li) / den
    br = bre_ref[...]
    bi = bim_ref[...]
    bb_re = f_re * br - f_im * bi
    bb_im = f_re * bi + f_im * br
    c_re = cre_ref[...]
    c_im = cim_ref[...]
    a2_ref[0] = ab_re * ab_re - ab_im * ab_im
    a2_ref[1] = 2.0 * ab_re * ab_im
    drive_ref[0, 0] = ab_re * bb_re - ab_im * bb_im
    drive_ref[0, 1] = bb_re
    drive_ref[1, 0] = ab_re * bb_im + ab_im * bb_re
    drive_ref[1, 1] = bb_im
    read_ref[0] = c_re
    read_ref[1] = c_im
    read_ref[2] = c_re * ab_re - c_im * ab_im
    read_ref[3] = c_re * ab_im + c_im * ab_re
    lane = lax.broadcasted_iota(jnp.int32, cbt_ref.shape, 2)
    cbt = jnp.zeros(cbt_ref.shape, F32)
    for h in range(cbt_ref.shape[2]):
        col = jnp.sum(c_re[:, h:h + 1, :] * bb_re - c_im[:, h:h + 1, :] * bb_im,
                      axis=-1, keepdims=True)
        cbt = jnp.where(lane == h, col, cbt)
    cbt_ref[...] = cbt


def _ssm_prep(lam_re_log, lam_im, log_dt, b_re, b_im, c_re, c_im):
    g, n, h = b_re.shape
    per_state = lambda a: a.reshape(g, 1, n)
    ldt = jnp.broadcast_to(log_dt[:, None, None], (g, 1, n))
    full = lambda shape: pl.BlockSpec(shape, lambda: (0,) * len(shape))
    out_shapes = [(2, g, 1, n), (2, 2, g, h, n), (4, g, h, n), (g, h, h)]
    return pl.pallas_call(
        _ssm_prep_kernel,
        in_specs=[full((g, 1, n))] * 3 + [full((g, h, n))] * 4,
        out_specs=[full(s) for s in out_shapes],
        out_shape=[jax.ShapeDtypeStruct(s, F32) for s in out_shapes],
        name="ssm_prep",
    )(per_state(lam_re_log), per_state(lam_im), ldt,
      b_re.transpose(0, 2, 1), b_im.transpose(0, 2, 1), c_re, c_im)


def _cast_plan(rows, nsteps):
    chunk = BF16_TILE_ROWS
    while rows % chunk or rows // chunk > nsteps:
        chunk += BF16_TILE_ROWS
    return chunk, rows // chunk


def _zero_tile(v):
    bits = pltpu.bitcast(v, jnp.uint32)
    acc = None
    for r in range(0, bits.shape[0], SUBLANES):
        for c in range(0, bits.shape[1], LANES):
            t = bits[r:r + SUBLANES, c:c + LANES]
            acc = t if acc is None else acc | t
    half = jnp.uint32(16)
    return lax.shift_right_logical(lax.shift_right_logical(acc, half), half).astype(F32)


def _ffn_kernel(x_ref, xn_ref, mod_ref, g_ref, win_ref, wout_ref, gfin_ref, *rest,
                sublayer, slab_in, final_norm, cast_chunks, nmain):
    ncast = len(cast_chunks)
    cast_in, o_ref, cast_out = rest[:ncast], rest[ncast], rest[ncast + 1:2 * ncast + 1]
    ha_ref, hb_ref, pre_ref = rest[2 * ncast + 1:]
    bsz, tt, fc = SUBLANES, TILE_FRAMES, FFN_FC
    rows = bsz * tt
    nc = D_FF // fc
    row = lambda b, k: mod_ref[3 * sublayer + k, b:b + 1, :]
    seq_rows = lambda b: slice(b * tt, (b + 1) * tt)
    step = pl.program_id(0)

    for src, dst, nchunks in zip(cast_in, cast_out, cast_chunks):
        @pl.when(step < nchunks)
        def _(src=src, dst=dst):
            dst[...] = src[...].astype(BF16)

    def load_x(ref, b, lo, hi):
        if slab_in:
            return jnp.concatenate(
                [ref[s, pl.ds(b, tt, stride=bsz), :] for s in range(lo // LANES, hi // LANES)],
                axis=-1)
        return ref[b, :, lo:hi]

    def normalise(ref, h_ref, b):
        gs = g_ref[...] * (1.0 + row(b, 1))
        h = (_rms_scale(load_x(ref, b, 0, D_MODEL)) * gs + row(b, 0)).astype(BF16)
        h_ref[seq_rows(b), :] = h
        return h

    def finish(b):
        y = _rms_scale(pre_ref[seq_rows(b), :]) * gfin_ref[...]
        o_ref[b] = y
        return y

    @pl.when(step == 0)
    def _():
        for b in range(bsz):
            normalise(x_ref, ha_ref, b)
        if final_norm:
            pre_ref[...] = jnp.zeros_like(pre_ref)

    if not slab_in:
        for s in range(N_SLABS):
            for b in range(bsz):
                o_ref[s, b * PLANE_PITCH + tt:(b + 1) * PLANE_PITCH, :] = \
                    jnp.zeros((PLANE_PITCH - tt, LANES), F32)

    def main(h_cur, h_nxt):
        hb = h_cur[...]
        gl = []
        for c in range(nc):
            a = _dot(hb, win_ref[:, c * fc:(c + 1) * fc])
            bb = _dot(hb, win_ref[:, D_FF + c * fc:D_FF + (c + 1) * fc])
            if c < bsz:
                z = _zero_tile(normalise(xn_ref, h_nxt, c))
                if final_norm:
                    z = z + _zero_tile(finish(c))
                z = jnp.concatenate([z] * (fc // LANES), axis=1)
                a = (a.reshape(rows // SUBLANES, SUBLANES, fc) + z[None]).reshape(rows, fc)
            gl.append((a * jax.nn.sigmoid(a) * bb).astype(BF16))
        glu = jnp.concatenate(gl, axis=-1)
        for n in range(D_MODEL // fc):
            lo = n * fc
            r = _dot(glu, wout_ref[:, lo:lo + fc])
            for b in range(bsz):
                out = load_x(x_ref, b, lo, lo + fc) + (0.5 * row(b, 2)[:, lo:lo + fc]) * r[seq_rows(b)]
                if final_norm:
                    pre_ref[seq_rows(b), lo:lo + fc] = out
                elif slab_in:
                    o_ref[b, :, lo:lo + fc] = out
                else:
                    for s in range(fc // LANES):
                        o_ref[lo // LANES + s, b * PLANE_PITCH:b * PLANE_PITCH + tt, :] = \
                            out[:, s * LANES:(s + 1) * LANES]

    even = lax.rem(step, 2) == 0
    pl.when(jnp.logical_and(step < nmain, even))(lambda: main(ha_ref, hb_ref))
    pl.when(jnp.logical_and(step < nmain, jnp.logical_not(even)))(lambda: main(hb_ref, ha_ref))

    if final_norm:
        @pl.when(step == nmain)
        def _():
            for b in range(bsz):
                finish(b)


def _ffn(x, mod, g, w_in, w_out, g_final, *, sublayer, slab_in, final_norm, bsz, seq, cast=()):
    d, f, tt = D_MODEL, D_FF, TILE_FRAMES
    nmain = seq // tt
    lag = 1 if final_norm else 0
    nsteps = nmain + lag
    cur = lambda i: jnp.minimum(i, nmain - 1)
    nxt = lambda i: jnp.minimum(i + 1, nmain - 1)
    done = lambda i: jnp.maximum(i - lag, 0)
    if slab_in:
        x_block = (N_SLABS, tt * bsz, LANES)
        x_specs = [pl.BlockSpec(x_block, lambda i: (0, cur(i), 0)),
                   pl.BlockSpec(x_block, lambda i: (0, nxt(i), 0))]
        out_spec = pl.BlockSpec((bsz, tt, d), lambda i: (0, done(i), 0))
        out_shape = jax.ShapeDtypeStruct((bsz, seq, d), F32)
    else:
        x_block = (bsz, tt, d)
        x_specs = [pl.BlockSpec(x_block, lambda i: (0, cur(i), 0)),
                   pl.BlockSpec(x_block, lambda i: (0, nxt(i), 0))]
        out_spec = pl.BlockSpec((N_SLABS, None, bsz * PLANE_PITCH, LANES),
                                lambda i: (0, done(i), 0, 0))
        out_shape = jax.ShapeDtypeStruct((N_SLABS, nmain, bsz * PLANE_PITCH, LANES), F32)
    plans = [_cast_plan(w.shape[0], nsteps) for w in cast]
    cast_specs = [pl.BlockSpec((chunk, w.shape[1]), lambda i, last=n - 1: (jnp.minimum(i, last), 0))
                  for w, (chunk, n) in zip(cast, plans)]
    outs = pl.pallas_call(
        functools.partial(_ffn_kernel, sublayer=sublayer, slab_in=slab_in, final_norm=final_norm,
                          cast_chunks=tuple(n for _, n in plans), nmain=nmain),
        grid=(nsteps,),
        in_specs=x_specs + [
            _resident((N_MOD_ROWS, bsz, d)),
            _resident((1, d)),
            _resident((d, 2 * f)),
            _resident((f, d)),
            _resident((1, d)),
        ] + cast_specs,
        out_specs=[out_spec] + cast_specs,
        out_shape=[out_shape] + [jax.ShapeDtypeStruct(w.shape, BF16) for w in cast],
        scratch_shapes=[
            pltpu.VMEM((tt * bsz, d), BF16),
            pltpu.VMEM((tt * bsz, d), BF16),
            pltpu.VMEM((tt * bsz if final_norm else SUBLANES, d), F32),
        ],
        compiler_params=pltpu.CompilerParams(
            dimension_semantics=("arbitrary",), vmem_limit_bytes=VMEM_LIMIT_BYTES),
        name=f"ffn{sublayer}",
    )(x, x, mod, g.reshape(1, d), w_in, w_out, g_final.reshape(1, d), *cast)
    return outs[0], tuple(outs[1:])


def _mixer_kernel(x_ref, mod_ref, g_ref, win_ref, pw_ref, pb_ref, ps_ref, wpu_ref,
                  drv_ref, a2_ref, rd_ref, cbt_ref, dsk_ref, wglu_ref, bglu_ref, wsu_ref, wout_ref,
                  o_ref, hist_ref, state_ref, st_ref):
    step = pl.program_id(0)
    bsz, tt, d = SUBLANES, TILE_FRAMES, D_MODEL
    rows = tt * bsz
    npair = tt // SCAN_FRAMES
    prow = npair * bsz
    halo = HALO_FRAMES * bsz
    shift, scale, gate = (mod_ref[3 * MIX_SUBLAYER + k] for k in range(3))
    tile3 = lambda v: v.reshape(tt, bsz, v.shape[-1])
    flat = lambda v: v.reshape(rows, v.shape[-1])

    @pl.when(step == 0)
    def _():
        state_ref[...] = jnp.zeros_like(state_ref)
        hist_ref[0:halo, :] = jnp.zeros((halo, POOL_WIDTH), F32)

    x = jnp.concatenate(
        [jnp.concatenate([x_ref[s, pl.ds(t, bsz, stride=PLANE_PITCH), :] for t in range(tt)], axis=0)
         for s in range(N_SLABS)], axis=-1)
    gs = g_ref[...] * (1.0 + scale)
    hb = flat(tile3(_rms_scale(x)) * gs[None] + shift[None]).astype(BF16)
    u_pool = _dot(hb, win_ref[:, 0:POOL_WIDTH])
    u_ssm = _dot(hb, win_ref[:, POOL_WIDTH:POOL_WIDTH + SSM_WIDTH])
    g0 = POOL_WIDTH + SSM_WIDTH

    hist_ref[halo:halo + rows, :] = u_pool
    frame = step * tt + lax.shift_right_logical(
        lax.broadcasted_iota(jnp.int32, (rows, 1), 0), int(math.log2(bsz)))
    zs = []
    for gi, w in enumerate(POOL_WINDOWS):
        lo = gi * POOL_GROUP
        cur = u_pool[:, lo:lo + POOL_GROUP]
        tot = cur
        for j in range(1, w):
            tot = tot + hist_ref[halo - j * bsz:halo - j * bsz + rows, lo:lo + POOL_GROUP]
        cnt = jnp.minimum(frame + 1, w).astype(F32)
        z = (tot / cnt - cur).astype(BF16)
        pz = _dot(z, pw_ref[gi]) + pb_ref[:, lo:lo + POOL_GROUP]
        zs.append(pz * ps_ref[:, lo:lo + POOL_GROUP])
    hist_ref[0:halo, :] = hist_ref[rows:rows + halo, :]
    pooled = jnp.concatenate(zs, axis=-1).astype(BF16)
    merged = jax.nn.sigmoid(_dot(hb, win_ref[:, g0:g0 + d])) * _dot(pooled, wpu_ref[...])
    gl_ssm = jax.nn.sigmoid(_dot(hb, win_ref[:, g0 + d:g0 + 2 * d]))

    u4 = u_ssm.reshape(npair, SCAN_FRAMES, bsz, SSM_WIDTH)
    ue = u4[:, 0].reshape(prow, SSM_WIDTH).astype(BF16)
    uo = u4[:, 1].reshape(prow, SSM_WIDTH).astype(BF16)
    kw = drv_ref.shape[2] // SCAN_FRAMES
    st_ref[0:bsz, :] = state_ref[...]
    for j in range(drv_ref.shape[1]):
        k0 = (j * MXU_DIM // (SSM_STATE // SSM_GROUP)) // kw * kw
        uk = jnp.concatenate([ue[:, k0:k0 + kw], uo[:, k0:k0 + kw]], axis=1)
        st_ref[bsz:bsz + prow, j * MXU_DIM:(j + 1) * MXU_DIM] = _dot(uk, drv_ref[0, j])
        st_ref[bsz:bsz + prow, N_STATES + j * MXU_DIM:N_STATES + (j + 1) * MXU_DIM] = \
            _dot(uk, drv_ref[1, j])

    for c in range(N_STATES // SCAN_LANES):
        re = slice(c * SCAN_LANES, (c + 1) * SCAN_LANES)
        im = slice(N_STATES + c * SCAN_LANES, N_STATES + (c + 1) * SCAN_LANES)
        a_re, a_im = a2_ref[0, :, re], a2_ref[1, :, re]
        s_re, s_im = state_ref[:, re], state_ref[:, im]
        for k in range(npair):
            fr = slice((k + 1) * bsz, (k + 2) * bsz)
            n_re = a_re * s_re - a_im * s_im + st_ref[fr, re]
            n_im = a_re * s_im + a_im * s_re + st_ref[fr, im]
            st_ref[fr, re] = n_re
            st_ref[fr, im] = n_im
            s_re, s_im = n_re, n_im
        state_ref[:, re] = s_re
        state_ref[:, im] = s_im

    kc = rd_ref.shape[2]
    yo, ye = [], []
    for j in range(rd_ref.shape[1]):
        cre = slice(j * kc, (j + 1) * kc)
        cim = slice(N_STATES + j * kc, N_STATES + (j + 1) * kc)
        so_re = st_ref[bsz:bsz + prow, cre].astype(BF16)
        so_im = st_ref[bsz:bsz + prow, cim].astype(BF16)
        sp_re = st_ref[0:prow, cre].astype(BF16)
        sp_im = st_ref[0:prow, cim].astype(BF16)
        yo.append(_dot(so_re, rd_ref[0, j]) - _dot(so_im, rd_ref[1, j]))
        ye.append(_dot(sp_re, rd_ref[2, j]) - _dot(sp_im, rd_ref[3, j])
                  + _dot(ue[:, j * MXU_DIM:(j + 1) * MXU_DIM], cbt_ref[j]))
    yo = jnp.concatenate(yo, axis=-1).reshape(npair, 1, bsz, SSM_WIDTH)
    ye = jnp.concatenate(ye, axis=-1).reshape(npair, 1, bsz, SSM_WIDTH)
    y = jnp.concatenate([ye, yo], axis=1).reshape(rows, SSM_WIDTH) + dsk_ref[...] * u_ssm
    y = 0.5 * y * (1.0 + lax.erf(y * (1.0 / math.sqrt(2.0))))
    vg = _dot(y.astype(BF16), wglu_ref[...]) + bglu_ref[...]
    glu = vg[:, 0:SSM_WIDTH] * jax.nn.sigmoid(vg[:, SSM_WIDTH:2 * SSM_WIDTH])
    merged = merged + gl_ssm * _dot(glu.astype(BF16), wsu_ref[...])
    mix = _dot(merged.astype(BF16), wout_ref[...])
    out = x + flat(gate[None] * tile3(mix))
    for s in range(N_SLABS):
        o_ref[s] = out[:, s * LANES:(s + 1) * LANES]


def _drive_tiles(drive):
    c, f, g, h, n = drive.shape
    gpt = MXU_DIM // n
    gpw = B_TILE_K // h
    tpw = gpw // gpt
    ntile = g // gpt
    wide = drive.reshape(c, f, ntile, gpt, h, n).transpose(0, 2, 1, 4, 3, 5)
    wide = wide.reshape(c, ntile, f, 1, h, gpt * n)
    slot = jnp.arange(gpw)[None, :, None, None]
    own = (gpt * (jnp.arange(ntile) % tpw)[:, None, None, None]
           + (jnp.arange(gpt * n) // n)[None, None, None, :])
    tiles = jnp.where((slot == own)[None, :, None], wide, 0)
    return tiles.reshape(c, ntile, f * B_TILE_K, MXU_DIM).astype(BF16)


def _readout_tiles(read):
    x, g, h, n = read.shape
    nt = SSM_WIDTH // MXU_DIM
    gpt = g // nt
    rows = read.reshape(x, nt, gpt, h, n).transpose(0, 1, 2, 4, 3).reshape(x, nt, gpt * n, h)
    return _spread_blocks(rows, n)


def _spread_blocks(rows, rows_per_block):
    h = rows.shape[-1]
    cols = jnp.arange(MXU_DIM)[None, :]
    copy = (jnp.arange(h)[:, None] == cols % h).astype(BF16)
    wide = jnp.einsum("...rh,hc->...rc", rows.astype(BF16), copy, preferred_element_type=F32)
    keep = jnp.arange(rows.shape[-2])[:, None] // rows_per_block == cols // h
    return jnp.where(keep, wide, 0).astype(BF16)


def _skip_tiles(cbt):
    g, k, h = cbt.shape
    nt = SSM_WIDTH // MXU_DIM
    return _spread_blocks(cbt.reshape(nt, g // nt * k, h), k)


def _mixer(x_planes, mod, g_mix, w_in, pool_w, pool_b, pool_scale, w_pool_up, ssm,
           d_skip, w_glu, b_glu, w_ssm_up, w_out, *, bsz, seq):
    d, tt = D_MODEL, TILE_FRAMES
    rows = tt * bsz
    a2, drive, read, cbt = ssm
    args = (
        x_planes, mod, g_mix.reshape(1, d), w_in, pool_w.astype(BF16),
        pool_b.reshape(1, POOL_WIDTH), pool_scale.reshape(1, POOL_WIDTH), w_pool_up,
        _drive_tiles(drive), jnp.broadcast_to(a2.reshape(2, 1, N_STATES), (2, bsz, N_STATES)),
        _readout_tiles(read), _skip_tiles(cbt), d_skip.reshape(1, SSM_WIDTH),
        w_glu, b_glu.reshape(1, 2 * SSM_WIDTH), w_ssm_up, w_out,
    )
    plane_spec = pl.BlockSpec((N_SLABS, None, bsz * PLANE_PITCH, LANES), lambda i: (0, i, 0, 0))
    slab_spec = pl.BlockSpec((N_SLABS, rows, LANES), lambda i: (0, i, 0))
    return pl.pallas_call(
        _mixer_kernel,
        grid=(seq // tt,),
        in_specs=[plane_spec] + [_resident(a.shape) for a in args[1:]],
        out_specs=slab_spec,
        out_shape=jax.ShapeDtypeStruct((N_SLABS, seq * bsz, LANES), F32),
        scratch_shapes=[
            pltpu.VMEM((rows + HALO_FRAMES * bsz, POOL_WIDTH), F32),
            pltpu.VMEM((bsz, 2 * N_STATES), F32),
            pltpu.VMEM((bsz + rows // SCAN_FRAMES, 2 * N_STATES), F32),
        ],
        compiler_params=pltpu.CompilerParams(
            dimension_semantics=("arbitrary",), vmem_limit_bytes=VMEM_LIMIT_BYTES),
        name="mixer",
    )(*args)


def kernel(x, c, w_ada, b_ada, g_ffn1, w_ffn1_in, w_ffn1_out, g_mix, w_in, pool_w, pool_b,
           pool_scale, w_pool_up, ssm_lam_re_log, ssm_lam_im, ssm_log_dt, ssm_b_re, ssm_b_im,
           ssm_c_re, ssm_c_im, ssm_d, w_glu, b_glu, w_ssm_up, w_out, g_ffn2, w_ffn2_in,
           w_ffn2_out, g_final):
    bsz, seq, d = x.shape
    assert (bsz, d) == (SUBLANES, D_MODEL) and seq % TILE_FRAMES == 0
    assert w_ada.shape[0] == 1 and w_ada.shape[2] == N_MOD_ROWS * d, "single-layer block"
    l = 0
    mod = _adaln(c, w_ada[l], b_ada[l])
    ssm = _ssm_prep(ssm_lam_re_log[l], ssm_lam_im[l], ssm_log_dt[l], ssm_b_re[l], ssm_b_im[l],
                    ssm_c_re[l], ssm_c_im[l])
    later = (w_in[l], w_pool_up[l], w_glu[l], w_ssm_up[l], w_out[l], w_ffn2_in[l], w_ffn2_out[l])
    x1, later_bf16 = _ffn(x, mod, g_ffn1[l], w_ffn1_in[l].astype(BF16), w_ffn1_out[l].astype(BF16),
                          g_final, sublayer=0, slab_in=False, final_norm=False, bsz=bsz, seq=seq,
                          cast=later)
    w_in_b, w_pool_up_b, w_glu_b, w_ssm_up_b, w_out_b, w_ffn2_in_b, w_ffn2_out_b = later_bf16
    x2 = _mixer(x1, mod, g_mix[l], w_in_b, pool_w[l], pool_b[l], pool_scale[l], w_pool_up_b,
                ssm, ssm_d[l],
                w_glu_b, b_glu[l], w_ssm_up_b, w_out_b, bsz=bsz, seq=seq)
    out, _ = _ffn(x2, mod, g_ffn2[l], w_ffn2_in_b, w_ffn2_out_b, g_final,
                  sublayer=2, slab_in=True, final_norm=True, bsz=bsz, seq=seq)
    return out
```
